```python
import jax, jax.numpy as jnp
from jax import lax
import numpy as np

D_MODEL = 1024
BATCH = 8
SEQ = 4096
DEPTH = 1

ROPE_THETA = 500000.0
NORM_EPS = 1e-5
NEG_INF = -1e30
POS_INF = 1e30
Q_BLOCK = 128

MLA_HEADS = 8
MLA_NOPE_DIM = 64
MLA_ROPE_DIM = 32
MLA_V_DIM = 64
MLA_Q_RANK = 256
MLA_KV_RANK = 128

NSA_HEADS = 8
NSA_KV_HEADS = 2
NSA_GROUP = NSA_HEADS // NSA_KV_HEADS
NSA_HEAD_DIM = 64
NSA_ROPE_DIM = NSA_HEAD_DIM // 4
CMP_BLOCK = 32
CMP_STRIDE = 16
CMP_HIDDEN = 2 * NSA_HEAD_DIM
SEL_BLOCK = 64
N_SEL = 16
N_LOCAL_SEL = 2
WINDOW = 512
NSA_Q_BLOCK = 64

MIX_WIDTH = MLA_HEADS * MLA_V_DIM + NSA_HEADS * NSA_HEAD_DIM
NSA_KV_WIDTH = NSA_KV_HEADS * NSA_HEAD_DIM
IN_SPLITS = (MLA_Q_RANK, MLA_KV_RANK, MLA_ROPE_DIM, NSA_HEADS * NSA_HEAD_DIM) + (NSA_KV_WIDTH,) * 6 + (3 * NSA_HEADS,)
IN_COLS = sum(IN_SPLITS)

N_EXPERTS = 32
TOP_K = 4
D_FF = D_MODEL
SWIGLU_LIMIT = 7.0
SWIGLU_ALPHA = 1.702
MOE_BLOCK = 128

kernel_name = 'hybrid_mla_nsa_moe_layer'


def rms_norm(x, g):
    xf = x.astype(jnp.float32)
    y = xf * lax.rsqrt(jnp.mean(xf * xf, axis=-1, keepdims=True) + NORM_EPS)
    return (y * g.astype(jnp.float32)).astype(x.dtype)


def rope_cos_sin(pos, rot_dim):
    inv_freq = ROPE_THETA ** (-jnp.arange(0, rot_dim, 2, dtype=jnp.float32) / rot_dim)
    ang = pos.astype(jnp.float32)[..., None] * inv_freq
    return jnp.cos(ang), jnp.sin(ang)


def apply_rope(x, cos, sin):
    half = cos.shape[-1]
    c = cos[:, :, None, :].astype(x.dtype)
    s = sin[:, :, None, :].astype(x.dtype)
    x1, x2, rest = x[..., :half], x[..., half:2 * half], x[..., 2 * half:]
    return jnp.concatenate([x1 * c - x2 * s, x2 * c + x1 * s, rest], axis=-1)


def masked_softmax(s, mask):
    return jax.nn.softmax(jnp.where(mask, s, NEG_INF), axis=-1)


def mla_mixer(q_lat, kv_lat, k_pe, cos, sin, q_norm, w_q_up, kv_norm, w_kv_up):
    B, S, _ = q_lat.shape
    q = (rms_norm(q_lat, q_norm) @ w_q_up).reshape(B, S, MLA_HEADS, MLA_NOPE_DIM + MLA_ROPE_DIM)
    kv = (rms_norm(kv_lat, kv_norm) @ w_kv_up).reshape(B, S, MLA_HEADS, MLA_NOPE_DIM + MLA_V_DIM)
    k_nope, v = kv[..., :MLA_NOPE_DIM], kv[..., MLA_NOPE_DIM:]
    q = jnp.concatenate([q[..., :MLA_NOPE_DIM], apply_rope(q[..., MLA_NOPE_DIM:], cos, sin)], axis=-1)
    k_pe = apply_rope(k_pe[:, :, None, :], cos, sin)
    k = jnp.concatenate([k_nope, jnp.broadcast_to(k_pe, (B, S, MLA_HEADS, MLA_ROPE_DIM))], axis=-1)
    scale = (MLA_NOPE_DIM + MLA_ROPE_DIM) ** -0.5
    k_pos = jnp.arange(S)

    def block(i):
        q0 = i * Q_BLOCK
        qb = lax.dynamic_slice_in_dim(q, q0, Q_BLOCK, axis=1)
        s = jnp.einsum('bqhd,bkhd->bhqk', qb, k).astype(jnp.float32) * scale
        q_pos = q0 + jnp.arange(Q_BLOCK)
        p = masked_softmax(s, k_pos[None, :] <= q_pos[:, None])
        return jnp.einsum('bhqk,bkhd->bqhd', p.astype(v.dtype), v)

    o = lax.map(block, jnp.arange(S // Q_BLOCK))
    return jnp.moveaxis(o, 0, 1).reshape(B, S, MLA_HEADS * MLA_V_DIM)


def compress(kv, pos_emb, w1, w2):
    S = kv.shape[1]
    n_cmp = (S - CMP_BLOCK) // CMP_STRIDE + 1
    idx = jnp.arange(n_cmp)[:, None] * CMP_STRIDE + jnp.arange(CMP_BLOCK)[None, :]
    blocks = kv[:, idx] + pos_emb[:, None, :]
    hid = jax.nn.silu(jnp.einsum('bnlgd,ldh->bngh', blocks, w1))
    return hid @ w2


def nsa_mixer(q, k_c, v_c, k_s, v_s, k_w, v_w, gate_logits, pos, cos, sin, cmp_pos, cmp_w1, cmp_w2):
    B, S, _ = q.shape
    G, Hg, dh = NSA_KV_HEADS, NSA_GROUP, NSA_HEAD_DIM
    scale = dh ** -0.5
    q = apply_rope(q.reshape(B, S, NSA_HEADS, dh), cos, sin).reshape(B, S, G, Hg, dh)
    kv_shape = (B, S, G, dh)
    k_c, v_c, k_s, v_s, k_w, v_w = [t.reshape(kv_shape) for t in (k_c, v_c, k_s, v_s, k_w, v_w)]
    k_s = apply_rope(k_s, cos, sin)
    k_w = apply_rope(k_w, cos, sin)
    t = jnp.arange(S)

    n_cmp = (S - CMP_BLOCK) // CMP_STRIDE + 1
    cmp_start = jnp.arange(n_cmp) * CMP_STRIDE
    cmp_end = cmp_start + CMP_BLOCK - 1
    kc = compress(k_c, cmp_pos[0], cmp_w1[0], cmp_w2[0])
    vc = compress(v_c, cmp_pos[1], cmp_w1[1], cmp_w2[1])
    ccos, csin = rope_cos_sin(pos[:, cmp_end], NSA_ROPE_DIM)
    kc = apply_rope(kc, ccos, csin)
    s_cmp = jnp.einsum('bsghd,bngd->bghsn', q, kc).astype(jnp.float32) * scale
    p_cmp = masked_softmax(s_cmp, cmp_end[None, :] <= t[:, None]) * (t >= CMP_BLOCK - 1)[:, None]
    o_cmp = jnp.einsum('bghsn,bngd->bsghd', p_cmp.astype(vc.dtype), vc)

    n_sel = S // SEL_BLOCK
    j = jnp.arange(n_sel)
    overlap = ((cmp_start[:, None] < (j[None, :] + 1) * SEL_BLOCK)
               & (cmp_start[:, None] + CMP_BLOCK > j[None, :] * SEL_BLOCK)).astype(jnp.float32)
    imp = jnp.einsum('bghsn,nj->bgsj', p_cmp, overlap)
    cur = (t // SEL_BLOCK)[:, None]
    forced = (j[None, :] == 0) | ((j[None, :] <= cur) & (j[None, :] > cur - N_LOCAL_SEL))
    imp = jnp.where(forced, POS_INF, jnp.where(j[None, :] <= cur, imp, NEG_INF))
    k_top = min(N_SEL, n_sel)
    _, sel_idx = lax.top_k(imp, k_top)

    ks_blocks = k_s.reshape(B, n_sel, SEL_BLOCK, G, dh).transpose(0, 3, 1, 2, 4)
    vs_blocks = v_s.reshape(B, n_sel, SEL_BLOCK, G, dh).transpose(0, 3, 1, 2, 4)
    gather = jax.vmap(jax.vmap(lambda blk, ix: blk[ix]))
    k_w_pad = jnp.pad(k_w, ((0, 0), (WINDOW, 0), (0, 0), (0, 0)))
    v_w_pad = jnp.pad(v_w, ((0, 0), (WINDOW, 0), (0, 0), (0, 0)))
    n_keys_sel = k_top * SEL_BLOCK

    def block(i):
        q0 = i * NSA_Q_BLOCK
        qb = lax.dynamic_slice_in_dim(q, q0, NSA_Q_BLOCK, axis=1)
        q_pos = q0 + jnp.arange(NSA_Q_BLOCK)
        ib = lax.dynamic_slice_in_dim(sel_idx, q0, NSA_Q_BLOCK, axis=2)
        kg = gather(ks_blocks, ib).reshape(B, G, NSA_Q_BLOCK, n_keys_sel, dh)
        vg = gather(vs_blocks, ib).reshape(B, G, NSA_Q_BLOCK, n_keys_sel, dh)
        key_pos = (ib[..., None] * SEL_BLOCK + jnp.arange(SEL_BLOCK)).reshape(B, G, NSA_Q_BLOCK, n_keys_sel)
        s_sel = jnp.einsum('bqghd,bgqkd->bghqk', qb, kg).astype(jnp.float32) * scale
        p_sel = masked_softmax(s_sel, (key_pos <= q_pos[:, None])[:, :, None])
        o_sel = jnp.einsum('bghqk,bgqkd->bqghd', p_sel.astype(vg.dtype), vg)
        kwb = lax.dynamic_slice_in_dim(k_w_pad, q0, NSA_Q_BLOCK + WINDOW, axis=1)
        vwb = lax.dynamic_slice_in_dim(v_w_pad, q0, NSA_Q_BLOCK + WINDOW, axis=1)
        w_pos = q0 - WINDOW + jnp.arange(NSA_Q_BLOCK + WINDOW)
        w_mask = ((w_pos[None, :] <= q_pos[:, None]) & (w_pos[None, :] > q_pos[:, None] - WINDOW)
                  & (w_pos[None, :] >= 0))
        s_win = jnp.einsum('bqghd,bkgd->bghqk', qb, kwb).astype(jnp.float32) * scale
        p_win = masked_softmax(s_win, w_mask)
        o_win = jnp.einsum('bghqk,bkgd->bqghd', p_win.astype(vwb.dtype), vwb)
        return o_sel, o_win

    o_sel, o_win = lax.map(block, jnp.arange(S // NSA_Q_BLOCK))
    o_sel = jnp.moveaxis(o_sel, 0, 1).reshape(B, S, G, Hg, dh)
    o_win = jnp.moveaxis(o_win, 0, 1).reshape(B, S, G, Hg, dh)
    gates = jax.nn.sigmoid(gate_logits.astype(jnp.float32)).reshape(B, S, G, Hg, 3).astype(q.dtype)
    o = gates[..., 0:1] * o_cmp + gates[..., 1:2] * o_sel + gates[..., 2:3] * o_win
    return o.reshape(B, S, NSA_HEADS * dh)


def moe_ffn(h, router_w, router_b, w1, b1, w2, b2):
    B, S, D = h.shape
    xt = h.reshape(-1, D)
    N = xt.shape[0]
    logits = (xt @ router_w + router_b).astype(jnp.float32)
    top_val, top_idx = lax.top_k(logits, TOP_K)
    gate = jax.nn.softmax(top_val, axis=-1)
    flat_e = top_idx.reshape(-1)
    flat_tok = jnp.repeat(jnp.arange(N, dtype=jnp.int32), TOP_K)
    order = jnp.argsort(flat_e)
    se, stok, sgate = flat_e[order], flat_tok[order], gate.reshape(-1)[order]
    counts = jnp.bincount(flat_e, length=N_EXPERTS)
    padded = ((counts + MOE_BLOCK - 1) // MOE_BLOCK) * MOE_BLOCK
    start = jnp.cumsum(counts) - counts
    pend = jnp.cumsum(padded)
    pstart = pend - padded
    dest = pstart[se] + (jnp.arange(N * TOP_K) - start[se])
    n_blocks = -(-(N * TOP_K) // MOE_BLOCK) + N_EXPERTS
    n_rows = n_blocks * MOE_BLOCK
    row_tok = jnp.zeros((n_rows,), jnp.int32).at[dest].set(stok)
    row_gate = jnp.zeros((n_rows,), xt.dtype).at[dest].set(sgate.astype(xt.dtype))
    block_expert = jnp.minimum(jnp.searchsorted(pend, jnp.arange(n_blocks) * MOE_BLOCK, side='right'),
                               N_EXPERTS - 1)

    def expert_block(args):
        tok, e, g = args
        hcat = xt[tok] @ w1[e] + b1[e]
        a = jnp.minimum(hcat[:, :D_FF], SWIGLU_LIMIT)
        up = jnp.clip(hcat[:, D_FF:], -SWIGLU_LIMIT, SWIGLU_LIMIT)
        glu = a * jax.nn.sigmoid(SWIGLU_ALPHA * a)
        return (((up + 1.0) * glu) @ w2[e] + b2[e]) * g[:, None]

    y = lax.map(expert_block, (row_tok.reshape(n_blocks, MOE_BLOCK), block_expert,
                               row_gate.reshape(n_blocks, MOE_BLOCK)))
    out = jax.ops.segment_sum(y.reshape(n_rows, D), row_tok, num_segments=N)
    return out.reshape(B, S, D)


def setup_inputs(seed: int = 0) -> dict:
    key = jax.random.key(seed)
    ks = jax.random.split(key, 24)
    f32 = jnp.float32
    L = DEPTH

    def normal(k, shape, scale):
        return jax.random.normal(k, shape, f32) * scale

    def gain(k, shape):
        return 1.0 + 0.02 * jax.random.normal(k, shape, f32)

    x = normal(ks[0], (BATCH, SEQ, D_MODEL), 1.0)
    positions = (jnp.arange(SEQ, dtype=jnp.int32)[None, :]
                 + jax.random.randint(ks[1], (BATCH, 1), 0, 1024, dtype=jnp.int32))
    return {
        'x': x,
        'positions': positions,
        'attn_norm': gain(ks[2], (L, D_MODEL)),
        'w_in': normal(ks[3], (L, D_MODEL, IN_COLS), D_MODEL ** -0.5),
        'mla_q_norm': gain(ks[4], (L, MLA_Q_RANK)),
        'mla_w_q_up': normal(ks[5], (L, MLA_Q_RANK, MLA_HEADS * (MLA_NOPE_DIM + MLA_ROPE_DIM)), MLA_Q_RANK ** -0.5),
        'mla_kv_norm': gain(ks[6], (L, MLA_KV_RANK)),
        'mla_w_kv_up': normal(ks[7], (L, MLA_KV_RANK, MLA_HEADS * (MLA_NOPE_DIM + MLA_V_DIM)), MLA_KV_RANK ** -0.5),
        'nsa_cmp_pos': normal(ks[8], (L, 2, CMP_BLOCK, NSA_HEAD_DIM), 0.1),
        'nsa_cmp_w1': normal(ks[9], (L, 2, CMP_BLOCK, NSA_HEAD_DIM, CMP_HIDDEN), (CMP_BLOCK * NSA_HEAD_DIM) ** -0.5),
        'nsa_cmp_w2': normal(ks[10], (L, 2, CMP_HIDDEN, NSA_HEAD_DIM), CMP_HIDDEN ** -0.5),
        'w_out': normal(ks[11], (L, MIX_WIDTH, D_MODEL), MIX_WIDTH ** -0.5),
        'ffn_norm': gain(ks[12], (L, D_MODEL)),
        'router_w': normal(ks[13], (L, D_MODEL, N_EXPERTS), D_MODEL ** -0.5),
        'router_b': normal(ks[14], (L, N_EXPERTS), 0.01),
        'moe_w1': normal(ks[15], (L, N_EXPERTS, D_MODEL, 2 * D_FF), D_MODEL ** -0.5),
        'moe_b1': normal(ks[16], (L, N_EXPERTS, 2 * D_FF), 0.02),
        'moe_w2': normal(ks[17], (L, N_EXPERTS, D_FF, D_MODEL), D_FF ** -0.5),
        'moe_b2': normal(ks[18], (L, N_EXPERTS, D_MODEL), 0.02),
        'final_norm': gain(ks[19], (D_MODEL,)),
    }


def reference(x, positions, attn_norm, w_in, mla_q_norm, mla_w_q_up, mla_kv_norm, mla_w_kv_up,
              nsa_cmp_pos, nsa_cmp_w1, nsa_cmp_w2, w_out, ffn_norm, router_w, router_b,
              moe_w1, moe_b1, moe_w2, moe_b2, final_norm):
    cos_mla, sin_mla = rope_cos_sin(positions, MLA_ROPE_DIM)
    cos_nsa, sin_nsa = rope_cos_sin(positions, NSA_ROPE_DIM)
    split_at = np.cumsum(IN_SPLITS)[:-1].tolist()
    h = x
    for l in range(DEPTH):
        u = rms_norm(h, attn_norm[l])
        (q_lat, kv_lat, k_pe, q_nsa, k_c, v_c, k_s, v_s, k_w, v_w, g_nsa) = jnp.split(u @ w_in[l], split_at, axis=-1)
        o_mla = mla_mixer(q_lat, kv_lat, k_pe, cos_mla, sin_mla,
                          mla_q_norm[l], mla_w_q_up[l], mla_kv_norm[l], mla_w_kv_up[l])
        o_nsa = nsa_mixer(q_nsa, k_c, v_c, k_s, v_s, k_w, v_w, g_nsa, positions, cos_nsa, sin_nsa,
                          nsa_cmp_pos[l], nsa_cmp_w1[l], nsa_cmp_w2[l])
        h = h + jnp.concatenate([o_mla, o_nsa], axis=-1) @ w_out[l]
        h = h + moe_ffn(rms_norm(h, ffn_norm[l]), router_w[l], router_b[l],
                        moe_w1[l], moe_b1[l], moe_w2[l], moe_b2[l])
    return rms_norm(h, final_norm)
```

```python
import functools

import jax
import jax.numpy as jnp
from jax import lax
from jax.experimental import pallas as pl
from jax.experimental.pallas import tpu as pltpu

F32 = jnp.float32
BF16 = jnp.bfloat16
I32 = jnp.int32

D_MODEL = 1024
ROPE_THETA = 500000.0
NORM_EPS = 1e-5
NEG_INF = -1e30
POS_INF = 1e30

MLA_HEADS = 8
MLA_NOPE_DIM = 64
MLA_ROPE_DIM = 32
MLA_V_DIM = 64
MLA_Q_RANK = 256
MLA_KV_RANK = 128

NSA_HEADS = 8
NSA_KV_HEADS = 2
NSA_GROUP = NSA_HEADS // NSA_KV_HEADS
NSA_HEAD_DIM = 64
NSA_ROPE_DIM = NSA_HEAD_DIM // 4
CMP_BLOCK = 32
CMP_STRIDE = 16
CMP_HIDDEN = 2 * NSA_HEAD_DIM
SEL_BLOCK = 64
N_SEL = 16
N_LOCAL_SEL = 2
WINDOW = 512

N_EXPERTS = 32
TOP_K = 4
D_FF = D_MODEL
SWIGLU_LIMIT = 7.0
SWIGLU_ALPHA = 1.702

LANES = 128
HALF = LANES // 2
VMEM_LIMIT = 48 * 1024 * 1024

PROJ_TM = 256
ATT_T = 512
CMP_TQ = 256
POST_TM = 256
FFN_BM = 512
COMB_TM = 128

C_QLAT = 0
C_KVLAT = C_QLAT + MLA_Q_RANK
C_KPE = C_KVLAT + MLA_KV_RANK
C_QNSA = C_KPE + LANES
C_KC = C_QNSA + NSA_HEADS * NSA_HEAD_DIM
C_VC = C_KC + LANES
C_KS = C_VC + LANES
C_VS = C_KS + LANES
C_KW = C_VS + LANES
C_VW = C_KW + LANES
C_GATE = C_VW + LANES
C_TOTAL = C_GATE + LANES


def _cparams(sem):
    return pltpu.CompilerParams(dimension_semantics=sem, vmem_limit_bytes=VMEM_LIMIT)


def _rms(x, g):
    return x * lax.rsqrt(jnp.mean(x * x, axis=-1, keepdims=True) + NORM_EPS) * g


def _sigmoid(x):
    return 1.0 / (1.0 + jnp.exp(-x))


def _rope(x, c, s, half, first):
    n = x.shape[-1]
    partner = jnp.where(first, pltpu.roll(x, n - half, 1), pltpu.roll(x, half, 1))
    return x * c + partner * s


def _expand_pair(x, lo, fill):
    y = pltpu.roll(x, HALF, 1)
    return (jnp.where(lo, x, fill), jnp.where(lo, fill, y), jnp.where(lo, y, fill), jnp.where(lo, fill, x))


def _proj_kernel(x_ref, mc_ref, ms_ref, nc_ref, ns_ref, an_ref, win_ref, qn_ref, wq_ref, kvn_ref, wk_ref,
                 wv_ref, qm_ref, km_ref, vm_ref, qs_ref, kc_ref, vc_ref, ks_ref, vs_ref, kw_ref, vw_ref,
                 g_ref, *, tm, seq):
    x = x_ref[...]
    u = _rms(x, an_ref[...])
    y = jnp.dot(u.astype(BF16), win_ref[...], preferred_element_type=F32)
    lane = lax.broadcasted_iota(I32, (1, LANES), 1)
    lo = lane < HALF
    mla_first = lane < MLA_NOPE_DIM + MLA_ROPE_DIM // 2
    nsa_first = (lane % NSA_HEAD_DIM) < NSA_ROPE_DIM // 2
    mc, ms, nc, ns = mc_ref[...], ms_ref[...], nc_ref[...], ns_ref[...]

    qn = _rms(y[:, C_QLAT:C_QLAT + MLA_Q_RANK], qn_ref[...])
    q = jnp.dot(qn.astype(BF16), wq_ref[...], preferred_element_type=F32)
    q = q * ((MLA_NOPE_DIM + MLA_ROPE_DIM) ** -0.5)
    for h in range(MLA_HEADS):
        sl = slice(h * LANES, (h + 1) * LANES)
        qm_ref[:, sl] = _rope(q[:, sl], mc, ms, MLA_ROPE_DIM // 2, mla_first).astype(BF16)
    kvn = _rms(y[:, C_KVLAT:C_KVLAT + MLA_KV_RANK], kvn_ref[...]).astype(BF16)
    kpe = _rope(y[:, C_KPE:C_KPE + LANES], mc, ms, MLA_ROPE_DIM // 2, mla_first)
    kn = jnp.dot(kvn, wk_ref[...], preferred_element_type=F32)
    for h in range(MLA_HEADS):
        sl = slice(h * LANES, (h + 1) * LANES)
        km_ref[:, sl] = (kn[:, sl] + kpe).astype(BF16)
    vm_ref[...] = jnp.dot(kvn, wv_ref[...], preferred_element_type=F32).astype(BF16)

    for c in range(NSA_HEADS * NSA_HEAD_DIM // LANES):
        ch = y[:, C_QNSA + c * LANES:C_QNSA + (c + 1) * LANES]
        qs_ref[:, c * LANES:(c + 1) * LANES] = (
            _rope(ch, nc, ns, NSA_ROPE_DIM // 2, nsa_first) * (NSA_HEAD_DIM ** -0.5)).astype(BF16)
    kc_ref[...] = y[:, C_KC:C_KC + LANES]
    vc_ref[...] = y[:, C_VC:C_VC + LANES]
    row = lax.broadcasted_iota(I32, (tm, LANES), 0)
    tok = (pl.program_id(0) * tm + row) % seq
    onehot = ((lane % HALF) == tok // SEL_BLOCK).astype(F32)
    ks = _rope(y[:, C_KS:C_KS + LANES], nc, ns, NSA_ROPE_DIM // 2, nsa_first)
    kw = _rope(y[:, C_KW:C_KW + LANES], nc, ns, NSA_ROPE_DIM // 2, nsa_first)
    for ref, val, fill in ((ks_ref, ks, onehot), (kw_ref, kw, 0.0),
                           (vs_ref, y[:, C_VS:C_VS + LANES], 0.0), (vw_ref, y[:, C_VW:C_VW + LANES], 0.0)):
        for c, chunk in enumerate(_expand_pair(val, lo, fill)):
            ref[:, c * LANES:(c + 1) * LANES] = chunk.astype(BF16)
    g_ref[...] = y[:, C_GATE:C_GATE + LANES]


def _proj(x2, tabs, an, win, qn, wq, kvn, wk, wv, seq):
    n = x2.shape[0]
    tm = PROJ_TM
    tok = lambda w: pl.BlockSpec((tm, w), lambda i: (i, 0))
    full = lambda a: pl.BlockSpec(a.shape, lambda i: (0,) * a.ndim)
    outs = [(8 * LANES, BF16), (8 * LANES, BF16), (8 * LANES, BF16), (4 * LANES, BF16), (LANES, F32), (LANES, F32),
            (4 * LANES, BF16), (4 * LANES, BF16), (4 * LANES, BF16), (4 * LANES, BF16), (LANES, F32)]
    return pl.pallas_call(
        functools.partial(_proj_kernel, tm=tm, seq=seq),
        grid=(n // tm,),
        in_specs=[tok(D_MODEL)] + [tok(LANES)] * 4 + [full(a) for a in (an, win, qn, wq, kvn, wk, wv)],
        out_specs=[tok(w) for w, _ in outs],
        out_shape=[jax.ShapeDtypeStruct((n, w), d) for w, d in outs],
        compiler_params=_cparams(("parallel",)),
        name="proj",
    )(x2, *tabs, an, win, qn, wq, kvn, wk, wv)


def _compress_kernel(kin_ref, vin_ref, pos_ref, w1a_ref, w1b_ref, w2_ref, cc_ref, cs_ref, kc_ref, vc_ref, *, rows):
    lane = lax.broadcasted_iota(I32, (1, LANES), 1)
    lo = lane < HALF
    nsa_first = (lane % NSA_HEAD_DIM) < NSA_ROPE_DIM // 2

    def comp(x, i):
        p = jnp.dot((x + pos_ref[i, 0:1, :]).astype(BF16), w1a_ref[i], preferred_element_type=F32)
        q = jnp.dot((x + pos_ref[i, 1:2, :]).astype(BF16), w1b_ref[i], preferred_element_type=F32)
        hid = p + pltpu.roll(q, rows - 1, 0)
        hid = hid * _sigmoid(hid)
        return jnp.dot(hid.astype(BF16), w2_ref[i], preferred_element_type=F32)

    kc = _rope(comp(kin_ref[0], 0), cc_ref[0], cs_ref[0], NSA_ROPE_DIM // 2, nsa_first)
    vc = comp(vin_ref[0], 1)
    for ref, val in ((kc_ref, kc), (vc_ref, vc)):
        for c, chunk in enumerate(_expand_pair(val, lo, 0.0)):
            ref[0, :, c * LANES:(c + 1) * LANES] = chunk.astype(BF16)


def _compress(kin, vin, pos, w1a, w1b, w2, cc, cs):
    b, rows, width = kin.shape
    per_b = lambda w: pl.BlockSpec((1, rows, w), lambda i: (i, 0, 0))
    full = lambda a: pl.BlockSpec(a.shape, lambda i: (0,) * a.ndim)
    return pl.pallas_call(
        functools.partial(_compress_kernel, rows=rows),
        grid=(b,),
        in_specs=[per_b(width), per_b(width), full(pos), full(w1a), full(w1b), full(w2), per_b(LANES), per_b(LANES)],
        out_specs=[per_b(4 * LANES), per_b(4 * LANES)],
        out_shape=[jax.ShapeDtypeStruct((b, rows, 4 * LANES), BF16)] * 2,
        compiler_params=_cparams(("parallel",)),
        name="compress",
    )(kin, vin, pos, w1a, w1b, w2, cc, cs)


def _flash_head(qa, kk, vv, mask, m_ref, l_ref, acc_ref, h, half_sel):
    s = lax.dot_general(qa, kk, (((1,), (1,)), ((), ())), preferred_element_type=F32)
    if mask is not None:
        s = jnp.where(mask, s, NEG_INF)
    m_prev = m_ref[h]
    m_new = jnp.maximum(m_prev, jnp.max(s, axis=-1, keepdims=True))
    alpha = jnp.exp(m_prev - m_new)
    p = jnp.exp(s - m_new)
    l_ref[h] = alpha * l_ref[h] + jnp.sum(p, axis=-1, keepdims=True)
    m_ref[h] = m_new
    pv = jnp.dot(p.astype(BF16), vv, preferred_element_type=F32)
    acc_ref[h // 2] = acc_ref[h // 2] * jnp.where(half_sel, alpha, 1.0) + pv


def _flash_init(m_ref, l_ref, acc_ref):
    m_ref[...] = jnp.full(m_ref.shape, NEG_INF, F32)
    l_ref[...] = jnp.zeros(l_ref.shape, F32)
    acc_ref[...] = jnp.zeros(acc_ref.shape, F32)


def _flash_out(l_ref, acc_ref, pair, lo):
    return acc_ref[pair] / jnp.where(lo, l_ref[2 * pair], l_ref[2 * pair + 1])


def _mla_kernel(q_ref, k_ref, v_ref, o_ref, m_ref, l_ref, acc_ref, *, t):
    qi, ki = pl.program_id(2), pl.program_id(3)
    lo = lax.broadcasted_iota(I32, (1, LANES), 1) < HALF

    @pl.when(ki == 0)
    def _():
        _flash_init(m_ref, l_ref, acc_ref)

    def step(mask):
        for h in range(2):
            sl = slice(h * LANES, (h + 1) * LANES)
            _flash_head(q_ref[0, :, sl], k_ref[0, :, sl], v_ref[0, :, sl], mask, m_ref, l_ref, acc_ref, h,
                        lo if h == 0 else jnp.logical_not(lo))

    @pl.when(ki < qi)
    def _():
        step(None)

    @pl.when(ki == qi)
    def _():
        r = lax.broadcasted_iota(I32, (t, t), 0)
        c = lax.broadcasted_iota(I32, (t, t), 1)
        step(c <= r)
        o_ref[0] = _flash_out(l_ref, acc_ref, 0, lo)


def _mla_attention(q, k, v):
    b, s, _ = q.shape
    t = min(ATT_T, s)
    pairs = MLA_HEADS // 2
    qspec = pl.BlockSpec((1, t, 2 * LANES), lambda bi, p, qi, ki: (bi, qi, p))
    kspec = pl.BlockSpec((1, t, 2 * LANES), lambda bi, p, qi, ki: (bi, jnp.minimum(ki, qi), p))
    return pl.pallas_call(
        functools.partial(_mla_kernel, t=t),
        grid=(b, pairs, s // t, s // t),
        in_specs=[qspec, kspec, kspec],
        out_specs=pl.BlockSpec((1, t, LANES), lambda bi, p, qi, ki: (bi, qi, p)),
        out_shape=jax.ShapeDtypeStruct((b, s, pairs * LANES), F32),
        scratch_shapes=[pltpu.VMEM((2, t, 1), F32), pltpu.VMEM((2, t, 1), F32), pltpu.VMEM((1, t, LANES), F32)],
        compiler_params=_cparams(("parallel", "parallel", "parallel", "arbitrary")),
        name="mla_attention",
    )(q, k, v)


def _cmp_kernel(q_ref, kc_ref, vc_ref, ov_ref, o_ref, qa_ref, vt_ref, *, tq, rows):
    qi = pl.program_id(2)
    lane = lax.broadcasted_iota(I32, (1, LANES), 1)
    lo = lane < HALF
    t_pos = qi * tq + lax.broadcasted_iota(I32, (tq, 1), 0)
    cmp_end = lax.broadcasted_iota(I32, (1, rows), 1) * CMP_STRIDE + (CMP_BLOCK - 1)
    mask = cmp_end <= t_pos
    live = (t_pos >= CMP_BLOCK - 1).astype(F32)

    psum = jnp.zeros((tq, rows), F32)
    for pair in range(NSA_GROUP // 2):
        qp = q_ref[0, :, pair * LANES:(pair + 1) * LANES]
        acc = jnp.zeros((tq, LANES), F32)
        for e in range(2):
            sl = slice(e * LANES, (e + 1) * LANES)
            s = lax.dot_general(qp, kc_ref[0, :, sl], (((1,), (1,)), ((), ())), preferred_element_type=F32)
            s = jnp.where(mask, s, NEG_INF)
            p = jnp.exp(s - jnp.max(s, axis=-1, keepdims=True))
            p = p / jnp.sum(p, axis=-1, keepdims=True) * live
            psum = psum + p
            acc = acc + jnp.dot(p.astype(BF16), vc_ref[0, :, sl], preferred_element_type=F32)
        o_ref[0, :, pair * LANES:(pair + 1) * LANES] = acc

    p_hi = psum.astype(BF16)
    p_lo = (psum - p_hi.astype(F32)).astype(BF16)
    imp = (jnp.dot(p_hi, ov_ref[...], preferred_element_type=F32)
           + jnp.dot(p_lo, ov_ref[...], preferred_element_type=F32))
    blk = lane % HALF
    cur = t_pos // SEL_BLOCK
    forced = (blk == 0) | ((blk <= cur) & (blk > cur - N_LOCAL_SEL))
    val = jnp.where(forced, POS_INF, jnp.where(blk <= cur, imp, NEG_INF))

    vt = val.T[:HALF]
    vt_ref[...] = vt
    jrow = lax.broadcasted_iota(I32, (HALF, tq), 0)

    def body(i, cnt):
        r = vt_ref[pl.ds(i, 1), :]
        ahead = (r > vt) | ((r == vt) & (i < jrow))
        return cnt + ahead.astype(I32)

    cnt = lax.fori_loop(0, HALF, body, jnp.zeros((HALF, tq), I32))
    bias_t = jnp.where(cnt < N_SEL, 0.0, NEG_INF).astype(F32)
    bias = jnp.concatenate([bias_t, bias_t], axis=0).T.astype(BF16)
    for h in range(NSA_GROUP):
        qp = q_ref[0, :, (h // 2) * LANES:(h // 2 + 1) * LANES]
        chunk = jnp.where(lo, qp, bias) if h % 2 == 0 else jnp.where(lo, bias, qp)
        qa_ref[0, :, h * LANES:(h + 1) * LANES] = chunk


def _cmp_attention(q, kcx, vcx, ov):
    b, s, _ = q.shape
    rows = kcx.shape[1]
    tq = min(CMP_TQ, s)
    g = NSA_KV_HEADS
    return pl.pallas_call(
        functools.partial(_cmp_kernel, tq=tq, rows=rows),
        grid=(b, g, s // tq),
        in_specs=[pl.BlockSpec((1, tq, 2 * LANES), lambda bi, gi, qi: (bi, qi, gi)),
                  pl.BlockSpec((1, rows, 2 * LANES), lambda bi, gi, qi: (bi, 0, gi)),
                  pl.BlockSpec((1, rows, 2 * LANES), lambda bi, gi, qi: (bi, 0, gi)),
                  pl.BlockSpec(ov.shape, lambda bi, gi, qi: (0, 0))],
        out_specs=[pl.BlockSpec((1, tq, 2 * LANES), lambda bi, gi, qi: (bi, qi, gi)),
                   pl.BlockSpec((1, tq, 4 * LANES), lambda bi, gi, qi: (bi, qi, gi))],
        out_shape=[jax.ShapeDtypeStruct((b, s, g * 2 * LANES), F32),
                   jax.ShapeDtypeStruct((b, s, g * 4 * LANES), BF16)],
        scratch_shapes=[pltpu.VMEM((HALF, tq), F32)],
        compiler_params=_cparams(("parallel", "parallel", "parallel")),
        name="cmp_attention",
    )(q, kcx, vcx, ov)


def _selwin_kernel(qa_ref, ks_ref, vs_ref, kw_ref, vw_ref, os_ref, ow_ref,
                   ms_ref, ls_ref, as_ref, mw_ref, lw_ref, aw_ref, *, t):
    qi, ki = pl.program_id(2), pl.program_id(3)
    lo = lax.broadcasted_iota(I32, (1, LANES), 1) < HALF

    @pl.when(ki == 0)
    def _():
        _flash_init(ms_ref, ls_ref, as_ref)
        _flash_init(mw_ref, lw_ref, aw_ref)

    def step(k_ref, v_ref, mask, m_ref, l_ref, acc_ref):
        for h in range(NSA_GROUP):
            kv = slice((h % 2) * LANES, (h % 2 + 1) * LANES)
            _flash_head(qa_ref[0, :, h * LANES:(h + 1) * LANES], k_ref[0, :, kv], v_ref[0, :, kv], mask,
                        m_ref, l_ref, acc_ref, h, lo if h % 2 == 0 else jnp.logical_not(lo))

    def rc():
        return lax.broadcasted_iota(I32, (t, t), 0), lax.broadcasted_iota(I32, (t, t), 1)

    @pl.when(ki < qi)
    def _():
        step(ks_ref, vs_ref, None, ms_ref, ls_ref, as_ref)

    @pl.when(ki == qi - 1)
    def _():
        r, c = rc()
        step(kw_ref, vw_ref, c - t > r - WINDOW, mw_ref, lw_ref, aw_ref)

    @pl.when(ki == qi)
    def _():
        r, c = rc()
        causal = c <= r
        step(ks_ref, vs_ref, causal, ms_ref, ls_ref, as_ref)
        step(kw_ref, vw_ref, causal & (c > r - WINDOW), mw_ref, lw_ref, aw_ref)
        for pair in range(NSA_GROUP // 2):
            sl = slice(pair * LANES, (pair + 1) * LANES)
            os_ref[0, :, sl] = _flash_out(ls_ref, as_ref, pair, lo)
            ow_ref[0, :, sl] = _flash_out(lw_ref, aw_ref, pair, lo)


def _selwin_attention(qa, ksx, vsx, kwx, vwx):
    b, s, _ = qa.shape
    t = min(ATT_T, s)
    assert t >= WINDOW or t == s, "window branch reads only the previous and the diagonal key tile"
    g = NSA_KV_HEADS
    pairs = NSA_GROUP // 2
    sel_kv = pl.BlockSpec((1, t, 2 * LANES), lambda bi, gi, qi, ki: (bi, jnp.minimum(ki, qi), gi))
    win_kv = pl.BlockSpec((1, t, 2 * LANES),
                          lambda bi, gi, qi, ki: (bi, jnp.clip(ki, jnp.maximum(qi - 1, 0), qi), gi))
    ospec = pl.BlockSpec((1, t, pairs * LANES), lambda bi, gi, qi, ki: (bi, qi, gi))
    stat = lambda: pltpu.VMEM((NSA_GROUP, t, 1), F32)
    accs = lambda: pltpu.VMEM((pairs, t, LANES), F32)
    return pl.pallas_call(
        functools.partial(_selwin_kernel, t=t),
        grid=(b, g, s // t, s // t),
        in_specs=[pl.BlockSpec((1, t, NSA_GROUP * LANES), lambda bi, gi, qi, ki: (bi, qi, gi)),
                  sel_kv, sel_kv, win_kv, win_kv],
        out_specs=[ospec, ospec],
        out_shape=[jax.ShapeDtypeStruct((b, s, g * pairs * LANES), F32)] * 2,
        scratch_shapes=[stat(), stat(), accs(), stat(), stat(), accs()],
        compiler_params=_cparams(("parallel", "parallel", "parallel", "arbitrary")),
        name="selwin_attention",
    )(qa, ksx, vsx, kwx, vwx)


def _post_kernel(x_ref, om_ref, oc_ref, os_ref, ow_ref, g_ref, eg_ref, wo_ref, fn_ref, rwh_ref, rwl_ref, rb_ref,
                 h_ref, xn_ref, route_ref, cnt_ref, *, tm):
    i = pl.program_id(0)
    width = NSA_HEADS * NSA_HEAD_DIM
    sg = _sigmoid(g_ref[...])
    sg_hi = sg.astype(BF16)
    sg_lo = (sg - sg_hi.astype(F32)).astype(BF16)
    ge = (jnp.dot(sg_hi, eg_ref[...], preferred_element_type=F32)
          + jnp.dot(sg_lo, eg_ref[...], preferred_element_type=F32))
    o_nsa = (ge[:, 0:width] * oc_ref[...] + ge[:, width:2 * width] * os_ref[...]
             + ge[:, 2 * width:3 * width] * ow_ref[...])
    mla_w = MLA_HEADS * MLA_V_DIM
    mixed = (jnp.dot(om_ref[...].astype(BF16), wo_ref[0:mla_w, :], preferred_element_type=F32)
             + jnp.dot(o_nsa.astype(BF16), wo_ref[mla_w:mla_w + width, :], preferred_element_type=F32))
    h = x_ref[...] + mixed
    h_ref[...] = h
    xn = _rms(h, fn_ref[...])
    xn_ref[...] = xn

    x_hi = xn.astype(BF16)
    x_lo = (xn - x_hi.astype(F32)).astype(BF16)
    logits = (jnp.dot(x_hi, rwh_ref[...], preferred_element_type=F32)
              + jnp.dot(x_hi, rwl_ref[...], preferred_element_type=F32)
              + jnp.dot(x_lo, rwh_ref[...], preferred_element_type=F32)) + rb_ref[...]
    lane = lax.broadcasted_iota(I32, (tm, LANES), 1)
    lg = jnp.where(lane < N_EXPERTS, logits, -jnp.inf)
    vals, hots = [], []
    for _ in range(TOP_K):
        m = jnp.max(lg, axis=-1, keepdims=True)
        idx = jnp.min(jnp.where(lg == m, lane, LANES), axis=-1, keepdims=True)
        hot = lane == idx
        lg = jnp.where(hot, -jnp.inf, lg)
        vals.append(m)
        hots.append(hot)
    es = [jnp.exp(v - vals[0]) for v in vals]
    den = es[0] + es[1] + es[2] + es[3]

    @pl.when(i == 0)
    def _():
        cnt_ref[...] = jnp.zeros(cnt_ref.shape, F32)

    hot_all = (hots[0] | hots[1] | hots[2] | hots[3]).astype(F32)
    r = lax.broadcasted_iota(I32, (tm, tm), 0)
    c = lax.broadcasted_iota(I32, (tm, tm), 1)
    tri = (c < r).astype(BF16)
    before = jnp.dot(tri, hot_all.astype(BF16), preferred_element_type=F32) + cnt_ref[...]
    route = jnp.zeros((tm, LANES), F32)
    for k in range(TOP_K):
        e_k = jnp.sum(jnp.where(hots[k], lane, 0), axis=-1, keepdims=True).astype(F32)
        rank_k = jnp.sum(jnp.where(hots[k], before, 0.0), axis=-1, keepdims=True)
        route = (route + jnp.where(lane == k, e_k, 0.0) + jnp.where(lane == TOP_K + k, es[k] / den, 0.0)
                 + jnp.where(lane == 2 * TOP_K + k, rank_k, 0.0))
    route_ref[...] = route
    cnt_ref[...] = cnt_ref[...] + jnp.sum(hot_all, axis=0, keepdims=True)


def _post(x2, om, oc, os_, ow, gl, eg, wo, fn, rwh, rwl, rb):
    n = x2.shape[0]
    tm = POST_TM
    tok = lambda w: pl.BlockSpec((tm, w), lambda i: (i, 0))
    full = lambda a: pl.BlockSpec(a.shape, lambda i: (0,) * a.ndim)
    return pl.pallas_call(
        functools.partial(_post_kernel, tm=tm),
        grid=(n // tm,),
        in_specs=[tok(D_MODEL), tok(om.shape[1]), tok(oc.shape[1]), tok(os_.shape[1]), tok(ow.shape[1]), tok(LANES)]
        + [full(a) for a in (eg, wo, fn, rwh, rwl, rb)],
        out_specs=[tok(D_MODEL), tok(D_MODEL), tok(LANES), pl.BlockSpec((1, LANES), lambda i: (0, 0))],
        out_shape=[jax.ShapeDtypeStruct((n, D_MODEL), F32), jax.ShapeDtypeStruct((n, D_MODEL), F32),
                   jax.ShapeDtypeStruct((n, LANES), F32), jax.ShapeDtypeStruct((1, LANES), F32)],
        compiler_params=_cparams(("arbitrary",)),
        name="post",
    )(x2, om, oc, os_, ow, gl, eg, wo, fn, rwh, rwl, rb)


def _gather_rows(idx_ref, src_hbm, dst, sem, count):
    def body(r, carry):
        pltpu.make_async_copy(src_hbm.at[pl.ds(idx_ref[0, 0, r], 1), :], dst.at[pl.ds(r, 1), :], sem).start()
        return carry
    lax.fori_loop(0, count, body, 0)


def _wait_rows(src_hbm, dst, sem, count):
    pltpu.make_async_copy(src_hbm.at[pl.ds(0, count), :], dst, sem).wait()


def _ffn_kernel(be_ref, bv_ref, idx0_ref, idxn_ref, xn_hbm, w1_ref, b1_ref, w2_ref, b2_ref, y_ref, xbuf, sem,
                *, bm, nb):
    del be_ref
    i = pl.program_id(0)
    slot = i % 2

    @pl.when(i == 0)
    def _():
        _gather_rows(idx0_ref, xn_hbm, xbuf.at[0], sem.at[0], bm)

    @pl.when((i + 1 < nb) & (bv_ref[jnp.minimum(i + 1, nb - 1)] > 0))
    def _():
        _gather_rows(idxn_ref, xn_hbm, xbuf.at[1 - slot], sem.at[1 - slot], bm)

    @pl.when(bv_ref[i] > 0)
    def _():
        _wait_rows(xn_hbm, xbuf.at[slot], sem.at[slot], bm)
        x = xbuf[slot].astype(BF16)
        hcat = jnp.dot(x, w1_ref[0], preferred_element_type=F32) + b1_ref[0]
        a = jnp.minimum(hcat[:, :D_FF], SWIGLU_LIMIT)
        up = jnp.clip(hcat[:, D_FF:], -SWIGLU_LIMIT, SWIGLU_LIMIT)
        glu = a * _sigmoid(SWIGLU_ALPHA * a)
        y_ref[...] = jnp.dot(((up + 1.0) * glu).astype(BF16), w2_ref[0], preferred_element_type=F32) + b2_ref[0]

    @pl.when(bv_ref[i] == 0)
    def _():
        y_ref[...] = jnp.zeros(y_ref.shape, F32)


def _ffn(block_expert, block_valid, row_tok, xn, w1, b1, w2, b2):
    nb = block_expert.shape[0]
    bm = FFN_BM
    idx3 = row_tok.reshape(nb, 1, bm)
    smem_blk = lambda fn: pl.BlockSpec((1, 1, bm), fn, memory_space=pltpu.SMEM)
    grid_spec = pltpu.PrefetchScalarGridSpec(
        num_scalar_prefetch=2,
        grid=(nb,),
        in_specs=[smem_blk(lambda i, be, bv: (0, 0, 0)),
                  smem_blk(lambda i, be, bv: (jnp.minimum(i + 1, nb - 1), 0, 0)),
                  pl.BlockSpec(memory_space=pl.ANY),
                  pl.BlockSpec((1, D_MODEL, 2 * D_FF), lambda i, be, bv: (be[i], 0, 0)),
                  pl.BlockSpec((1, 1, 2 * D_FF), lambda i, be, bv: (be[i], 0, 0)),
                  pl.BlockSpec((1, D_FF, D_MODEL), lambda i, be, bv: (be[i], 0, 0)),
                  pl.BlockSpec((1, 1, D_MODEL), lambda i, be, bv: (be[i], 0, 0))],
        out_specs=pl.BlockSpec((bm, D_MODEL), lambda i, be, bv: (i, 0)),
        scratch_shapes=[pltpu.VMEM((2, bm, D_MODEL), F32), pltpu.SemaphoreType.DMA((2,))],
    )
    return pl.pallas_call(
        functools.partial(_ffn_kernel, bm=bm, nb=nb),
        grid_spec=grid_spec,
        out_shape=jax.ShapeDtypeStruct((nb * bm, D_MODEL), F32),
        compiler_params=_cparams(("arbitrary",)),
        name="expert_ffn",
    )(block_expert, block_valid, idx3, idx3, xn, w1, b1, w2, b2)


def _combine_kernel(idx0_ref, idxn_ref, h_ref, route_ref, fn_ref, ys_hbm, o_ref, ybuf, sem, *, tm, nt):
    i = pl.program_id(0)
    slot = i % 2
    rows = TOP_K * tm

    @pl.when(i == 0)
    def _():
        _gather_rows(idx0_ref, ys_hbm, ybuf.at[0], sem.at[0], rows)

    @pl.when(i + 1 < nt)
    def _():
        _gather_rows(idxn_ref, ys_hbm, ybuf.at[1 - slot], sem.at[1 - slot], rows)

    _wait_rows(ys_hbm, ybuf.at[slot], sem.at[slot], rows)
    acc = h_ref[...]
    for k in range(TOP_K):
        acc = acc + route_ref[:, TOP_K + k:TOP_K + k + 1] * ybuf[slot, k * tm:(k + 1) * tm, :]
    o_ref[...] = _rms(acc, fn_ref[...])


def _combine(dest_t, h, route, fn, ys):
    n = h.shape[0]
    tm = COMB_TM
    nt = n // tm
    rows = TOP_K * tm
    smem_blk = lambda fn_: pl.BlockSpec((1, 1, rows), fn_, memory_space=pltpu.SMEM)
    return pl.pallas_call(
        functools.partial(_combine_kernel, tm=tm, nt=nt),
        grid=(nt,),
        in_specs=[smem_blk(lambda i: (0, 0, 0)), smem_blk(lambda i: (jnp.minimum(i + 1, nt - 1), 0, 0)),
                  pl.BlockSpec((tm, D_MODEL), lambda i: (i, 0)), pl.BlockSpec((tm, LANES), lambda i: (i, 0)),
                  pl.BlockSpec((1, D_MODEL), lambda i: (0, 0)), pl.BlockSpec(memory_space=pl.ANY)],
        out_specs=pl.BlockSpec((tm, D_MODEL), lambda i: (i, 0)),
        out_shape=jax.ShapeDtypeStruct((n, D_MODEL), F32),
        scratch_shapes=[pltpu.VMEM((2, rows, D_MODEL), F32), pltpu.SemaphoreType.DMA((2,))],
        compiler_params=_cparams(("arbitrary",)),
        name="combine",
    )(dest_t, dest_t, h, route, fn, ys)


def _rope_tables(positions, rot, lead, period):
    inv_freq = ROPE_THETA ** (-jnp.arange(0, rot, 2, dtype=F32) / rot)
    ang = positions.astype(F32)[..., None] * inv_freq
    cos, sin = jnp.cos(ang), jnp.sin(ang)
    shape = cos.shape[:-1]
    tail = period - lead - rot
    c = jnp.concatenate([jnp.ones(shape + (lead,), F32), cos, cos, jnp.ones(shape + (tail,), F32)], axis=-1)
    s = jnp.concatenate([jnp.zeros(shape + (lead,), F32), -sin, sin, jnp.zeros(shape + (tail,), F32)], axis=-1)
    reps = LANES // period
    return jnp.tile(c, reps), jnp.tile(s, reps)


def _layout_w_in(w):
    d = w.shape[0]
    sizes = (MLA_Q_RANK, MLA_KV_RANK, MLA_ROPE_DIM, NSA_HEADS * NSA_HEAD_DIM) + (NSA_KV_HEADS * NSA_HEAD_DIM,) * 6 \
        + (3 * NSA_HEADS,)
    offs = [0]
    for sz in sizes:
        offs.append(offs[-1] + sz)
    seg = [w[:, offs[j]:offs[j + 1]] for j in range(len(sizes))]
    z = lambda n: jnp.zeros((d, n), w.dtype)
    kpe = jnp.concatenate([z(MLA_NOPE_DIM), seg[2], z(LANES - MLA_NOPE_DIM - MLA_ROPE_DIM)], axis=1)
    gates = jnp.concatenate([seg[10], z(LANES - 3 * NSA_HEADS)], axis=1)
    out = jnp.concatenate([seg[0], seg[1], kpe, seg[3]] + seg[4:10] + [gates], axis=1)
    assert out.shape[1] == C_TOTAL
    return out


def _layout_mla_up(w_q_up, w_kv_up):
    rq = w_q_up.shape[0]
    qd = MLA_NOPE_DIM + MLA_ROPE_DIM
    wq = w_q_up.reshape(rq, MLA_HEADS, qd)
    wq = jnp.pad(wq, ((0, 0), (0, 0), (0, LANES - qd))).reshape(rq, MLA_HEADS * LANES)
    rk = w_kv_up.shape[0]
    wkv = w_kv_up.reshape(rk, MLA_HEADS, MLA_NOPE_DIM + MLA_V_DIM)
    wk = jnp.pad(wkv[:, :, :MLA_NOPE_DIM], ((0, 0), (0, 0), (0, LANES - MLA_NOPE_DIM))).reshape(rk, MLA_HEADS * LANES)
    wv = wkv[:, :, MLA_NOPE_DIM:].reshape(rk, MLA_HEADS // 2, 2, MLA_V_DIM)
    zero = jnp.zeros_like(wv[:, :, 0])
    wv = jnp.stack([wv[:, :, 0], zero, zero, wv[:, :, 1]], axis=2).reshape(rk, MLA_HEADS * LANES)
    return wq, wk, wv


def _layout_compress(cmp_pos, cmp_w1, cmp_w2):
    g = NSA_KV_HEADS
    eye = jnp.eye(g, dtype=F32)
    half = CMP_BLOCK // 2
    assert CMP_BLOCK == 2 * CMP_STRIDE

    def w1_part(w):
        return jnp.einsum('ldh,gk->lgdkh', w, eye).reshape(half * g * NSA_HEAD_DIM, g * CMP_HIDDEN)

    def pos_part(p):
        return jnp.broadcast_to(p[:, None, :], (half, g, NSA_HEAD_DIM)).reshape(1, -1)

    w1a = jnp.stack([w1_part(cmp_w1[i, :half]) for i in range(2)]).astype(BF16)
    w1b = jnp.stack([w1_part(cmp_w1[i, half:]) for i in range(2)]).astype(BF16)
    w2 = jnp.stack([jnp.einsum('hd,gk->ghkd', cmp_w2[i], eye).reshape(g * CMP_HIDDEN, g * NSA_HEAD_DIM)
                    for i in range(2)]).astype(BF16)
    pos = jnp.stack([jnp.concatenate([pos_part(cmp_pos[i, :half]), pos_part(cmp_pos[i, half:])], axis=0)
                     for i in range(2)])
    return pos, w1a, w1b, w2


def _overlap_matrix(rows, n_cmp):
    n = jnp.arange(rows)[:, None]
    j = jnp.arange(LANES)[None, :] % HALF
    start = n * CMP_STRIDE
    ov = (start < (j + 1) * SEL_BLOCK) & (start + CMP_BLOCK > j * SEL_BLOCK) & (n < n_cmp)
    return ov.astype(BF16)


def _gate_expand_matrix():
    width = NSA_HEADS * NSA_HEAD_DIM
    rows = jnp.arange(LANES)[:, None]
    cols = jnp.arange(3 * width)[None, :]
    branch, head = cols // width, (cols % width) // NSA_HEAD_DIM
    return (rows == head * 3 + branch).astype(BF16)


def kernel(x, positions, attn_norm, w_in, mla_q_norm, mla_w_q_up, mla_kv_norm, mla_w_kv_up, nsa_cmp_pos,
           nsa_cmp_w1, nsa_cmp_w2, w_out, ffn_norm, router_w, router_b, moe_w1, moe_b1, moe_w2, moe_b2, final_norm):
    b, s, d = x.shape
    n = b * s
    depth = attn_norm.shape[0]
    assert d == D_MODEL and s % SEL_BLOCK == 0 and s // SEL_BLOCK <= HALF and n % PROJ_TM == 0
    assert s % CMP_STRIDE == 0 and n % POST_TM == 0 and n % COMB_TM == 0

    mla_c, mla_s = _rope_tables(positions.reshape(n), MLA_ROPE_DIM, MLA_NOPE_DIM, LANES)
    nsa_c, nsa_s = _rope_tables(positions.reshape(n), NSA_ROPE_DIM, 0, NSA_HEAD_DIM)
    rows = s // CMP_STRIDE
    n_cmp = (s - CMP_BLOCK) // CMP_STRIDE + 1
    cmp_end = jnp.minimum(jnp.arange(rows) * CMP_STRIDE + CMP_BLOCK - 1, s - 1)
    cmp_c, cmp_s = _rope_tables(positions[:, cmp_end], NSA_ROPE_DIM, 0, NSA_HEAD_DIM)
    ov = _overlap_matrix(rows, n_cmp)
    eg = _gate_expand_matrix()

    h2 = x.reshape(n, d)
    for l in range(depth):
        win = _layout_w_in(w_in[l]).astype(BF16)
        wq, wk, wv = (w.astype(BF16) for w in _layout_mla_up(mla_w_q_up[l], mla_w_kv_up[l]))
        (qm, km, vm, qs, kc_in, vc_in, ksx, vsx, kwx, vwx, gl) = _proj(
            h2, (mla_c, mla_s, nsa_c, nsa_s), attn_norm[l][None], win, mla_q_norm[l][None], wq,
            mla_kv_norm[l][None], wk, wv, s)

        pos, w1a, w1b, w2c = _layout_compress(nsa_cmp_pos[l], nsa_cmp_w1[l], nsa_cmp_w2[l])
        flat = CMP_STRIDE * NSA_KV_HEADS * NSA_HEAD_DIM
        kcx, vcx = _compress(kc_in.reshape(b, rows, flat), vc_in.reshape(b, rows, flat), pos, w1a, w1b, w2c,
                             cmp_c, cmp_s)

        o_mla = _mla_attention(qm.reshape(b, s, -1), km.reshape(b, s, -1), vm.reshape(b, s, -1))
        o_cmp, qa = _cmp_attention(qs.reshape(b, s, -1), kcx, vcx, ov)
        o_sel, o_win = _selwin_attention(qa, ksx.reshape(b, s, -1), vsx.reshape(b, s, -1),
                                         kwx.reshape(b, s, -1), vwx.reshape(b, s, -1))

        rw = jnp.pad(router_w[l], ((0, 0), (0, LANES - N_EXPERTS)))
        rw_hi = rw.astype(BF16)
        rw_lo = (rw - rw_hi.astype(F32)).astype(BF16)
        rb = jnp.pad(router_b[l], (0, LANES - N_EXPERTS))[None]
        hres, xn, route, counts = _post(
            h2, o_mla.reshape(n, -1), o_cmp.reshape(n, -1), o_sel.reshape(n, -1), o_win.reshape(n, -1), gl, eg,
            w_out[l].astype(BF16), ffn_norm[l][None], rw_hi, rw_lo, rb)

        top_idx = route[:, 0:TOP_K].astype(I32)
        rank = route[:, 2 * TOP_K:3 * TOP_K].astype(I32)
        cnt = counts[0, :N_EXPERTS].astype(I32)
        padded = ((cnt + FFN_BM - 1) // FFN_BM) * FFN_BM
        pend = jnp.cumsum(padded)
        pstart = pend - padded
        dest = pstart[top_idx] + rank
        nb = -(-(n * TOP_K) // FFN_BM) + N_EXPERTS
        block_row = jnp.arange(nb, dtype=I32) * FFN_BM
        block_expert = jnp.minimum(jnp.searchsorted(pend, block_row, side='right'), N_EXPERTS - 1).astype(I32)
        block_valid = (block_row < pend[-1]).astype(I32)
        tok_ids = jnp.broadcast_to(jnp.arange(n, dtype=I32)[:, None], (n, TOP_K))
        row_tok = jnp.zeros((nb * FFN_BM,), I32).at[dest.reshape(-1)].set(tok_ids.reshape(-1), unique_indices=True)

        ys = _ffn(block_expert, block_valid, row_tok, xn, moe_w1[l].astype(BF16), moe_b1[l][:, None, :],
                  moe_w2[l].astype(BF16), moe_b2[l][:, None, :])
        dest_t = dest.reshape(n // COMB_TM, COMB_TM, TOP_K).transpose(0, 2, 1).reshape(n // COMB_TM, 1,
                                                                                      TOP_K * COMB_TM)
        fn = final_norm[None] if l == depth - 1 else jnp.ones((1, d), F32)
        h2 = _combine(dest_t, hres, route, fn, ys)
        if l != depth - 1:
            raise NotImplementedError("only the single-layer configuration is laid out here")
    return h2.reshape(b, s, d)
```

```python
import functools
import math

import jax
import jax.numpy as jnp
from jax import lax
from jax.experimental import pallas as pl
from jax.experimental.pallas import tpu as pltpu

F32 = jnp.float32
BF16 = jnp.bfloat16
I32 = jnp.int32

D_MODEL = 1024
ROPE_THETA = 500000.0
NORM_EPS = 1e-5
NEG_INF = -1e30
POS_INF = 1e30
LOG2E = math.log2(math.e)

MLA_HEADS = 8
MLA_NOPE_DIM = 64
MLA_ROPE_DIM = 32
MLA_V_DIM = 64
MLA_Q_RANK = 256
MLA_KV_RANK = 128

NSA_HEADS = 8
NSA_KV_HEADS = 2
NSA_GROUP = NSA_HEADS // NSA_KV_HEADS
NSA_HEAD_DIM = 64
NSA_ROPE_DIM = NSA_HEAD_DIM // 4
CMP_BLOCK = 32
CMP_STRIDE = 16
CMP_HIDDEN = 2 * NSA_HEAD_DIM
SEL_BLOCK = 64
N_SEL = 16
N_LOCAL_SEL = 2
WINDOW = 512

N_EXPERTS = 32
TOP_K = 4
D_FF = D_MODEL
SWIGLU_LIMIT = 7.0
SWIGLU_ALPHA = 1.702

LANES = 128
HALF = LANES // 2
VMEM_LIMIT = 48 * 1024 * 1024

PROJ_TM = 256
ATT_T = 512
ATT_QS = 256
ATT_KC_FULL = 512
ATT_KC_DIAG = 256
ATT_AHEAD = 3
CMP_TQ = 256
POST_TM = 256
FFN_BM = 512
COMB_TM = 128

C_QLAT = 0
C_KVLAT = C_QLAT + MLA_Q_RANK
C_KPE = C_KVLAT + MLA_KV_RANK
C_QNSA = C_KPE + LANES
C_KC = C_QNSA + NSA_HEADS * NSA_HEAD_DIM
C_VC = C_KC + LANES
C_KS = C_VC + LANES
C_VS = C_KS + LANES
C_KW = C_VS + LANES
C_VW = C_KW + LANES
C_GATE = C_VW + LANES
C_TOTAL = C_GATE + LANES


def _cparams(sem):
    return pltpu.CompilerParams(dimension_semantics=sem, vmem_limit_bytes=VMEM_LIMIT)


def _rms(x, g):
    return x * lax.rsqrt(jnp.mean(x * x, axis=-1, keepdims=True) + NORM_EPS) * g


def _sigmoid(x):
    return 1.0 / (1.0 + jnp.exp(-x))


def _rope(x, c, s, half, first):
    n = x.shape[-1]
    partner = jnp.where(first, pltpu.roll(x, n - half, 1), pltpu.roll(x, half, 1))
    return x * c + partner * s


def _expand_pair(x, lo, fill):
    y = pltpu.roll(x, HALF, 1)
    return (jnp.where(lo, x, fill), jnp.where(lo, fill, y), jnp.where(lo, y, fill), jnp.where(lo, fill, x))


def _proj_kernel(x_ref, mc_ref, ms_ref, nc_ref, ns_ref, an_ref, win_ref, qn_ref, wq_ref, kvn_ref, wk_ref,
                 wv_ref, qm_ref, km_ref, vm_ref, qs_ref, kc_ref, vc_ref, ks_ref, vs_ref, kw_ref, vw_ref,
                 g_ref, *, tm, seq):
    x = x_ref[...]
    u = _rms(x, an_ref[...])
    y = jnp.dot(u.astype(BF16), win_ref[...], preferred_element_type=F32)
    lane = lax.broadcasted_iota(I32, (1, LANES), 1)
    lo = lane < HALF
    mla_first = lane < MLA_NOPE_DIM + MLA_ROPE_DIM // 2
    nsa_first = (lane % NSA_HEAD_DIM) < NSA_ROPE_DIM // 2
    mc, ms, nc, ns = mc_ref[...], ms_ref[...], nc_ref[...], ns_ref[...]

    qn = _rms(y[:, C_QLAT:C_QLAT + MLA_Q_RANK], qn_ref[...])
    q = jnp.dot(qn.astype(BF16), wq_ref[...], preferred_element_type=F32)
    q = q * ((MLA_NOPE_DIM + MLA_ROPE_DIM) ** -0.5 * LOG2E)
    for h in range(MLA_HEADS):
        sl = slice(h * LANES, (h + 1) * LANES)
        qm_ref[:, sl] = _rope(q[:, sl], mc, ms, MLA_ROPE_DIM // 2, mla_first).astype(BF16)
    kvn = _rms(y[:, C_KVLAT:C_KVLAT + MLA_KV_RANK], kvn_ref[...]).astype(BF16)
    kpe = _rope(y[:, C_KPE:C_KPE + LANES], mc, ms, MLA_ROPE_DIM // 2, mla_first)
    kn = jnp.dot(kvn, wk_ref[...], preferred_element_type=F32)
    for h in range(MLA_HEADS):
        sl = slice(h * LANES, (h + 1) * LANES)
        km_ref[:, sl] = (kn[:, sl] + kpe).astype(BF16)
    vm_ref[0, 0] = jnp.dot(kvn, wv_ref[...], preferred_element_type=F32).T.astype(BF16)

    for c in range(NSA_HEADS * NSA_HEAD_DIM // LANES):
        ch = y[:, C_QNSA + c * LANES:C_QNSA + (c + 1) * LANES]
        qs_ref[:, c * LANES:(c + 1) * LANES] = (
            _rope(ch, nc, ns, NSA_ROPE_DIM // 2, nsa_first) * (NSA_HEAD_DIM ** -0.5 * LOG2E)).astype(BF16)
    kc_ref[...] = y[:, C_KC:C_KC + LANES]
    vc_ref[...] = y[:, C_VC:C_VC + LANES]
    row = lax.broadcasted_iota(I32, (tm, LANES), 0)
    tok = (pl.program_id(0) * tm + row) % seq
    onehot = ((lane % HALF) == tok // SEL_BLOCK).astype(F32)
    ks = _rope(y[:, C_KS:C_KS + LANES], nc, ns, NSA_ROPE_DIM // 2, nsa_first)
    kw = _rope(y[:, C_KW:C_KW + LANES], nc, ns, NSA_ROPE_DIM // 2, nsa_first)
    for ref, val, fill in ((ks_ref, ks, onehot), (kw_ref, kw, 0.0)):
        for c, chunk in enumerate(_expand_pair(val, lo, fill)):
            ref[:, c * LANES:(c + 1) * LANES] = chunk.astype(BF16)
    vs_ref[0, 0] = y[:, C_VS:C_VS + LANES].T.astype(BF16)
    vw_ref[0, 0] = y[:, C_VW:C_VW + LANES].T.astype(BF16)
    g_ref[...] = y[:, C_GATE:C_GATE + LANES]


def _proj(x2, tabs, an, win, qn, wq, kvn, wk, wv, batch, seq):
    n = x2.shape[0]
    tm = PROJ_TM
    per_b, per_slab = seq // tm, ATT_T // tm
    tok = lambda w: pl.BlockSpec((tm, w), lambda i: (i, 0))
    tr = lambda r: pl.BlockSpec((1, 1, r, tm),
                                lambda i: (i // per_b, (i % per_b) // per_slab, 0, (i % per_b) % per_slab))
    full = lambda a: pl.BlockSpec(a.shape, lambda i: (0,) * a.ndim)
    tok_out = lambda w, d: (tok(w), jax.ShapeDtypeStruct((n, w), d))
    tr_out = lambda r: (tr(r), jax.ShapeDtypeStruct((batch, seq // ATT_T, r, ATT_T), BF16))
    outs = [tok_out(8 * LANES, BF16), tok_out(8 * LANES, BF16), tr_out(MLA_HEADS * MLA_V_DIM),
            tok_out(4 * LANES, BF16), tok_out(LANES, F32), tok_out(LANES, F32),
            tok_out(4 * LANES, BF16), tr_out(LANES), tok_out(4 * LANES, BF16), tr_out(LANES), tok_out(LANES, F32)]
    return pl.pallas_call(
        functools.partial(_proj_kernel, tm=tm, seq=seq),
        grid=(n // tm,),
        in_specs=[tok(D_MODEL)] + [tok(LANES)] * 4 + [full(a) for a in (an, win, qn, wq, kvn, wk, wv)],
        out_specs=[o[0] for o in outs],
        out_shape=[o[1] for o in outs],
        compiler_params=_cparams(("parallel",)),
        name="proj",
    )(x2, *tabs, an, win, qn, wq, kvn, wk, wv)


def _compress_kernel(kin_ref, vin_ref, pos_ref, w1a_ref, w1b_ref, w2_ref, cc_ref, cs_ref, kc_ref, vc_ref, *, rows):
    lane = lax.broadcasted_iota(I32, (1, LANES), 1)
    lo = lane < HALF
    nsa_first = (lane % NSA_HEAD_DIM) < NSA_ROPE_DIM // 2

    def comp(x, i):
        p = jnp.dot((x + pos_ref[i, 0:1, :]).astype(BF16), w1a_ref[i], preferred_element_type=F32)
        q = jnp.dot((x + pos_ref[i, 1:2, :]).astype(BF16), w1b_ref[i], preferred_element_type=F32)
        hid = p + pltpu.roll(q, rows - 1, 0)
        hid = hid * _sigmoid(hid)
        return jnp.dot(hid.astype(BF16), w2_ref[i], preferred_element_type=F32)

    kc = _rope(comp(kin_ref[0], 0), cc_ref[0], cs_ref[0], NSA_ROPE_DIM // 2, nsa_first)
    vc = comp(vin_ref[0], 1)
    for ref, val in ((kc_ref, kc), (vc_ref, vc)):
        for c, chunk in enumerate(_expand_pair(val, lo, 0.0)):
            ref[0, :, c * LANES:(c + 1) * LANES] = chunk.astype(BF16)


def _compress(kin, vin, pos, w1a, w1b, w2, cc, cs):
    b, rows, width = kin.shape
    per_b = lambda w: pl.BlockSpec((1, rows, w), lambda i: (i, 0, 0))
    full = lambda a: pl.BlockSpec(a.shape, lambda i: (0,) * a.ndim)
    return pl.pallas_call(
        functools.partial(_compress_kernel, rows=rows),
        grid=(b,),
        in_specs=[per_b(width), per_b(width), full(pos), full(w1a), full(w1b), full(w2), per_b(LANES), per_b(LANES)],
        out_specs=[per_b(4 * LANES), per_b(4 * LANES)],
        out_shape=[jax.ShapeDtypeStruct((b, rows, 4 * LANES), BF16)] * 2,
        compiler_params=_cparams(("parallel",)),
        name="compress",
    )(kin, vin, pos, w1a, w1b, w2, cc, cs)


def _subtile_plan(t, kc, delta, window):
    plan = []
    for j in range(t // ATT_QS):
        for c in range(t // kc):
            if delta is None:
                plan.append((j, c, False))
                continue
            dmin = delta + j * ATT_QS - (c * kc + kc - 1)
            dmax = delta + j * ATT_QS + ATT_QS - 1 - c * kc
            if dmax < 0 or (window is not None and dmin >= window):
                continue
            full = dmin >= 0 and (window is None or dmax < window)
            plan.append((j, c, not full))
    return plan


def _flash_tile(q_of, k_of, vt_of, m_ref, l_ref, acc_ref, heads, t, kc, delta, window):
    plan = _subtile_plan(t, kc, delta, window)
    chains = [(h, j) for h in heads for j in sorted({jj for jj, _, _ in plan})]
    jsl = lambda j: slice(j * ATT_QS, (j + 1) * ATT_QS)
    state = {(h, j): (m_ref[h:h + 1, jsl(j)], l_ref[h:h + 1, jsl(j)], acc_ref[h, :, jsl(j)]) for h, j in chains}
    items = [(h, j, c, masked) for h, j in chains for jj, c, masked in plan if jj == j]

    def scores(item):
        h, j, c, masked = item
        st = lax.dot_general(k_of(h, c, kc), q_of(h, j), (((1,), (1,)), ((), ())), preferred_element_type=F32)
        if masked:
            kk = lax.broadcasted_iota(I32, (kc, ATT_QS), 0)
            qq = lax.broadcasted_iota(I32, (kc, ATT_QS), 1)
            d = (delta + j * ATT_QS - c * kc) + qq - kk
            ok = d >= 0
            if window is not None:
                ok = ok & (d < window)
            st = jnp.where(ok, st, NEG_INF)
        return st

    ahead = [scores(it) for it in items[:ATT_AHEAD]]
    pending = None
    for n, (h, j, c, _) in enumerate(items):
        st = ahead.pop(0)
        if n + ATT_AHEAD < len(items):
            ahead.append(scores(items[n + ATT_AHEAD]))
        m, l, acc = state[(h, j)]
        m_new = jnp.maximum(m, jnp.max(st, axis=0, keepdims=True))
        alpha = jnp.exp2(m - m_new)
        p = jnp.exp2(st - m_new)
        l = alpha * l + jnp.sum(p, axis=0, keepdims=True)
        pv = jnp.dot(vt_of(h, c, kc), p.astype(BF16), preferred_element_type=F32)
        if pending is not None:
            key, a_prev, pv_prev = pending
            mm, ll, aa = state[key]
            state[key] = (mm, ll, aa * a_prev + pv_prev)
            if key == (h, j):
                acc = state[key][2]
        state[(h, j)] = (m_new, l, acc)
        pending = ((h, j), alpha, pv)
    key, a_prev, pv_prev = pending
    mm, ll, aa = state[key]
    state[key] = (mm, ll, aa * a_prev + pv_prev)
    for h, j in chains:
        m, l, acc = state[(h, j)]
        m_ref[h:h + 1, jsl(j)] = m
        l_ref[h:h + 1, jsl(j)] = l
        acc_ref[h, :, jsl(j)] = acc


def _flash_init(m_ref, l_ref, acc_ref):
    m_ref[...] = jnp.full(m_ref.shape, NEG_INF, F32)
    l_ref[...] = jnp.zeros(l_ref.shape, F32)
    acc_ref[...] = jnp.zeros(acc_ref.shape, F32)


def _flash_out_pair(l_ref, acc_ref, pair):
    parts = [acc_ref[h] / l_ref[h:h + 1, :] for h in (2 * pair, 2 * pair + 1)]
    return jnp.concatenate(parts, axis=0).T


def _kv_readers(k_ref, vt_ref, kt, t, k_chunk_of_head, v_rows_of_head):
    def k_of(h, c, kc):
        start = pl.multiple_of(kt * t + c * kc, kc)
        return k_ref[0, pl.ds(start, kc), k_chunk_of_head(h)]

    def vt_of(h, c, kc):
        return vt_ref[0, kt, v_rows_of_head(h), c * kc:(c + 1) * kc]

    return k_of, vt_of


def _mla_kernel(q_ref, k_ref, vt_ref, o_ref, m_ref, l_ref, acc_ref, *, t):
    qi = pl.program_id(2)
    heads = (0, 1)
    chunk = lambda h: slice(h * LANES, (h + 1) * LANES)
    vrows = lambda h: slice(h * MLA_V_DIM, (h + 1) * MLA_V_DIM)
    q_of = lambda h, j: q_ref[0, j * ATT_QS:(j + 1) * ATT_QS, chunk(h)]
    _flash_init(m_ref, l_ref, acc_ref)

    def tile(kt, kc, delta):
        k_of, vt_of = _kv_readers(k_ref, vt_ref, kt, t, chunk, vrows)
        _flash_tile(q_of, k_of, vt_of, m_ref, l_ref, acc_ref, heads, t, kc, delta, None)

    def body(kt, carry):
        tile(kt, ATT_KC_FULL, None)
        return carry

    lax.fori_loop(0, qi, body, 0)
    tile(qi, ATT_KC_DIAG, 0)
    o_ref[0] = _flash_out_pair(l_ref, acc_ref, 0)


def _mla_attention(q, k, vt):
    b, s, _ = q.shape
    t = ATT_T
    pairs = MLA_HEADS // 2
    return pl.pallas_call(
        functools.partial(_mla_kernel, t=t),
        grid=(b, pairs, s // t),
        in_specs=[pl.BlockSpec((1, t, 2 * LANES), lambda bi, p, qi: (bi, qi, p)),
                  pl.BlockSpec((1, s, 2 * LANES), lambda bi, p, qi: (bi, 0, p)),
                  pl.BlockSpec((1, s // t, 2 * MLA_V_DIM, t), lambda bi, p, qi: (bi, 0, p, 0))],
        out_specs=pl.BlockSpec((1, t, LANES), lambda bi, p, qi: (bi, qi, p)),
        out_shape=jax.ShapeDtypeStruct((b, s, pairs * LANES), F32),
        scratch_shapes=[pltpu.VMEM((2, t), F32), pltpu.VMEM((2, t), F32), pltpu.VMEM((2, MLA_V_DIM, t), F32)],
        compiler_params=_cparams(("parallel", "parallel", "arbitrary")),
        name="mla_attention",
    )(q, k, vt)


def _cmp_kernel(q_ref, kc_ref, vc_ref, ov_ref, o_ref, qa_ref, vt_ref, *, tq, rows):
    qi = pl.program_id(2)
    lane = lax.broadcasted_iota(I32, (1, LANES), 1)
    lo = lane < HALF
    t_pos = qi * tq + lax.broadcasted_iota(I32, (tq, 1), 0)
    cmp_end = lax.broadcasted_iota(I32, (1, rows), 1) * CMP_STRIDE + (CMP_BLOCK - 1)
    mask = cmp_end <= t_pos
    live = (t_pos >= CMP_BLOCK - 1).astype(F32)

    psum = jnp.zeros((tq, rows), F32)
    for pair in range(NSA_GROUP // 2):
        qp = q_ref[0, :, pair * LANES:(pair + 1) * LANES]
        acc = jnp.zeros((tq, LANES), F32)
        for e in range(2):
            sl = slice(e * LANES, (e + 1) * LANES)
            s = lax.dot_general(qp, kc_ref[0, :, sl], (((1,), (1,)), ((), ())), preferred_element_type=F32)
            s = jnp.where(mask, s, NEG_INF)
            p = jnp.exp2(s - jnp.max(s, axis=-1, keepdims=True))
            p = p / jnp.sum(p, axis=-1, keepdims=True) * live
            psum = psum + p
            acc = acc + jnp.dot(p.astype(BF16), vc_ref[0, :, sl], preferred_element_type=F32)
        o_ref[0, :, pair * LANES:(pair + 1) * LANES] = acc

    p_hi = psum.astype(BF16)
    p_lo = (psum - p_hi.astype(F32)).astype(BF16)
    imp = (jnp.dot(p_hi, ov_ref[...], preferred_element_type=F32)
           + jnp.dot(p_lo, ov_ref[...], preferred_element_type=F32))
    blk = lane % HALF
    cur = t_pos // SEL_BLOCK
    forced = (blk == 0) | ((blk <= cur) & (blk > cur - N_LOCAL_SEL))
    val = jnp.where(forced, POS_INF, jnp.where(blk <= cur, imp, NEG_INF))

    vt = val.T[:HALF]
    vt_ref[...] = vt
    jrow = lax.broadcasted_iota(I32, (HALF, tq), 0)

    def body(i, cnt):
        r = vt_ref[pl.ds(i, 1), :]
        ahead = (r > vt) | ((r == vt) & (i < jrow))
        return cnt + ahead.astype(I32)

    cnt = lax.fori_loop(0, HALF, body, jnp.zeros((HALF, tq), I32))
    bias_t = jnp.where(cnt < N_SEL, 0.0, NEG_INF).astype(F32)
    bias = jnp.concatenate([bias_t, bias_t], axis=0).T.astype(BF16)
    for h in range(NSA_GROUP):
        qp = q_ref[0, :, (h // 2) * LANES:(h // 2 + 1) * LANES]
        chunk = jnp.where(lo, qp, bias) if h % 2 == 0 else jnp.where(lo, bias, qp)
        qa_ref[0, :, h * LANES:(h + 1) * LANES] = chunk


def _cmp_attention(q, kcx, vcx, ov):
    b, s, _ = q.shape
    rows = kcx.shape[1]
    tq = min(CMP_TQ, s)
    g = NSA_KV_HEADS
    return pl.pallas_call(
        functools.partial(_cmp_kernel, tq=tq, rows=rows),
        grid=(b, g, s // tq),
        in_specs=[pl.BlockSpec((1, tq, 2 * LANES), lambda bi, gi, qi: (bi, qi, gi)),
                  pl.BlockSpec((1, rows, 2 * LANES), lambda bi, gi, qi: (bi, 0, gi)),
                  pl.BlockSpec((1, rows, 2 * LANES), lambda bi, gi, qi: (bi, 0, gi)),
                  pl.BlockSpec(ov.shape, lambda bi, gi, qi: (0, 0))],
        out_specs=[pl.BlockSpec((1, tq, 2 * LANES), lambda bi, gi, qi: (bi, qi, gi)),
                   pl.BlockSpec((1, tq, 4 * LANES), lambda bi, gi, qi: (bi, qi, gi))],
        out_shape=[jax.ShapeDtypeStruct((b, s, g * 2 * LANES), F32),
                   jax.ShapeDtypeStruct((b, s, g * 4 * LANES), BF16)],
        scratch_shapes=[pltpu.VMEM((HALF, tq), F32)],
        compiler_params=_cparams(("parallel", "parallel", "parallel")),
        name="cmp_attention",
    )(q, kcx, vcx, ov)


def _selwin_kernel(qa_ref, ks_ref, vs_ref, kw_ref, vw_ref, os_ref, ow_ref,
                   ms_ref, ls_ref, as_ref, mw_ref, lw_ref, aw_ref, *, t):
    qi = pl.program_id(2)
    heads = tuple(range(NSA_GROUP))
    chunk = lambda h: slice((h % 2) * LANES, (h % 2 + 1) * LANES)
    vrows = lambda h: slice(0, NSA_HEAD_DIM)
    q_of = lambda h, j: qa_ref[0, j * ATT_QS:(j + 1) * ATT_QS, h * LANES:(h + 1) * LANES]
    _flash_init(ms_ref, ls_ref, as_ref)
    _flash_init(mw_ref, lw_ref, aw_ref)

    def sel_tile(kt, kc, delta):
        k_of, vt_of = _kv_readers(ks_ref, vs_ref, kt, t, chunk, vrows)
        _flash_tile(q_of, k_of, vt_of, ms_ref, ls_ref, as_ref, heads, t, kc, delta, None)

    def win_tile(kt, delta):
        k_of, vt_of = _kv_readers(kw_ref, vw_ref, kt, t, chunk, vrows)
        _flash_tile(q_of, k_of, vt_of, mw_ref, lw_ref, aw_ref, heads, t, ATT_KC_DIAG, delta, WINDOW)

    def body(kt, carry):
        sel_tile(kt, ATT_KC_FULL, None)
        return carry

    lax.fori_loop(0, qi, body, 0)

    @pl.when(qi > 0)
    def _():
        win_tile(qi - 1, t)

    sel_tile(qi, ATT_KC_DIAG, 0)
    win_tile(qi, 0)
    for pair in range(NSA_GROUP // 2):
        sl = slice(pair * LANES, (pair + 1) * LANES)
        os_ref[0, :, sl] = _flash_out_pair(ls_ref, as_ref, pair)
        ow_ref[0, :, sl] = _flash_out_pair(lw_ref, aw_ref, pair)


def _selwin_attention(qa, ksx, vst, kwx, vwt):
    b, s, _ = qa.shape
    t = ATT_T
    assert t >= WINDOW, "window branch reads only the previous and the diagonal key tile"
    g = NSA_KV_HEADS
    pairs = NSA_GROUP // 2
    kspec = pl.BlockSpec((1, s, 2 * LANES), lambda bi, gi, qi: (bi, 0, gi))
    vspec = pl.BlockSpec((1, s // t, NSA_HEAD_DIM, t), lambda bi, gi, qi: (bi, 0, gi, 0))
    ospec = pl.BlockSpec((1, t, pairs * LANES), lambda bi, gi, qi: (bi, qi, gi))
    stat = lambda: pltpu.VMEM((NSA_GROUP, t), F32)
    accs = lambda: pltpu.VMEM((NSA_GROUP, NSA_HEAD_DIM, t), F32)
    return pl.pallas_call(
        functools.partial(_selwin_kernel, t=t),
        grid=(b, g, s // t),
        in_specs=[pl.BlockSpec((1, t, NSA_GROUP * LANES), lambda bi, gi, qi: (bi, qi, gi)),
                  kspec, vspec, kspec, vspec],
        out_specs=[ospec, ospec],
        out_shape=[jax.ShapeDtypeStruct((b, s, g * pairs * LANES), F32)] * 2,
        scratch_shapes=[stat(), stat(), accs(), stat(), stat(), accs()],
        compiler_params=_cparams(("parallel", "parallel", "arbitrary")),
        name="selwin_attention",
    )(qa, ksx, vst, kwx, vwt)


def _post_kernel(x_ref, om_ref, oc_ref, os_ref, ow_ref, g_ref, eg_ref, wo_ref, fn_ref, rwh_ref, rwl_ref, rb_ref,
                 h_ref, xn_ref, route_ref, cnt_ref, *, tm):
    i = pl.program_id(0)
    width = NSA_HEADS * NSA_HEAD_DIM
    sg = _sigmoid(g_ref[...])
    sg_hi = sg.astype(BF16)
    sg_lo = (sg - sg_hi.astype(F32)).astype(BF16)
    ge = (jnp.dot(sg_hi, eg_ref[...], preferred_element_type=F32)
          + jnp.dot(sg_lo, eg_ref[...], preferred_element_type=F32))
    o_nsa = (ge[:, 0:width] * oc_ref[...] + ge[:, width:2 * width] * os_ref[...]
             + ge[:, 2 * width:3 * width] * ow_ref[...])
    mla_w = MLA_HEADS * MLA_V_DIM
    mixed = (jnp.dot(om_ref[...].astype(BF16), wo_ref[0:mla_w, :], preferred_element_type=F32)
             + jnp.dot(o_nsa.astype(BF16), wo_ref[mla_w:mla_w + width, :], preferred_element_type=F32))
    h = x_ref[...] + mixed
    h_ref[...] = h
    xn = _rms(h, fn_ref[...])
    xn_ref[...] = xn

    x_hi = xn.astype(BF16)
    x_lo = (xn - x_hi.astype(F32)).astype(BF16)
    logits = (jnp.dot(x_hi, rwh_ref[...], preferred_element_type=F32)
              + jnp.dot(x_hi, rwl_ref[...], preferred_element_type=F32)
              + jnp.dot(x_lo, rwh_ref[...], preferred_element_type=F32)) + rb_ref[...]
    lane = lax.broadcasted_iota(I32, (tm, LANES), 1)
    lg = jnp.where(lane < N_EXPERTS, logits, -jnp.inf)
    vals, hots = [], []
    for _ in range(TOP_K):
        m = jnp.max(lg, axis=-1, keepdims=True)
        idx = jnp.min(jnp.where(lg == m, lane, LANES), axis=-1, keepdims=True)
        hot = lane == idx
        lg = jnp.where(hot, -jnp.inf, lg)
        vals.append(m)
        hots.append(hot)
    es = [jnp.exp(v - vals[0]) for v in vals]
    den = es[0] + es[1] + es[2] + es[3]

    @pl.when(i == 0)
    def _():
        cnt_ref[...] = jnp.zeros(cnt_ref.shape, F32)

    hot_all = (hots[0] | hots[1] | hots[2] | hots[3]).astype(F32)
    r = lax.broadcasted_iota(I32, (tm, tm), 0)
    c = lax.broadcasted_iota(I32, (tm, tm), 1)
    tri = (c < r).astype(BF16)
    before = jnp.dot(tri, hot_all.astype(BF16), preferred_element_type=F32) + cnt_ref[...]
    route = jnp.zeros((tm, LANES), F32)
    for k in range(TOP_K):
        e_k = jnp.sum(jnp.where(hots[k], lane, 0), axis=-1, keepdims=True).astype(F32)
        rank_k = jnp.sum(jnp.where(hots[k], before, 0.0), axis=-1, keepdims=True)
        route = (route + jnp.where(lane == k, e_k, 0.0) + jnp.where(lane == TOP_K + k, es[k] / den, 0.0)
                 + jnp.where(lane == 2 * TOP_K + k, rank_k, 0.0))
    route_ref[...] = route
    cnt_ref[...] = cnt_ref[...] + jnp.sum(hot_all, axis=0, keepdims=True)


def _post(x2, om, oc, os_, ow, gl, eg, wo, fn, rwh, rwl, rb):
    n = x2.shape[0]
    tm = POST_TM
    tok = lambda w: pl.BlockSpec((tm, w), lambda i: (i, 0))
    full = lambda a: pl.BlockSpec(a.shape, lambda i: (0,) * a.ndim)
    return pl.pallas_call(
        functools.partial(_post_kernel, tm=tm),
        grid=(n // tm,),
        in_specs=[tok(D_MODEL), tok(om.shape[1]), tok(oc.shape[1]), tok(os_.shape[1]), tok(ow.shape[1]), tok(LANES)]
        + [full(a) for a in (eg, wo, fn, rwh, rwl, rb)],
        out_specs=[tok(D_MODEL), tok(D_MODEL), tok(LANES), pl.BlockSpec((1, LANES), lambda i: (0, 0))],
        out_shape=[jax.ShapeDtypeStruct((n, D_MODEL), F32), jax.ShapeDtypeStruct((n, D_MODEL), F32),
                   jax.ShapeDtypeStruct((n, LANES), F32), jax.ShapeDtypeStruct((1, LANES), F32)],
        compiler_params=_cparams(("arbitrary",)),
        name="post",
    )(x2, om, oc, os_, ow, gl, eg, wo, fn, rwh, rwl, rb)


def _gather_rows(idx_ref, src_hbm, dst, sem, count):
    def body(r, carry):
        pltpu.make_async_copy(src_hbm.at[pl.ds(idx_ref[0, 0, r], 1), :], dst.at[pl.ds(r, 1), :], sem).start()
        return carry
    lax.fori_loop(0, count, body, 0)


def _wait_rows(src_hbm, dst, sem, count):
    pltpu.make_async_copy(src_hbm.at[pl.ds(0, count), :], dst, sem).wait()


def _ffn_kernel(be_ref, bv_ref, idx0_ref, idxn_ref, xn_hbm, w1_ref, b1_ref, w2_ref, b2_ref, y_ref, xbuf, sem,
                *, bm, nb):
    del be_ref
    i = pl.program_id(0)
    slot = i % 2

    @pl.when(i == 0)
    def _():
        _gather_rows(idx0_ref, xn_hbm, xbuf.at[0], sem.at[0], bm)

    @pl.when((i + 1 < nb) & (bv_ref[jnp.minimum(i + 1, nb - 1)] > 0))
    def _():
        _gather_rows(idxn_ref, xn_hbm, xbuf.at[1 - slot], sem.at[1 - slot], bm)

    @pl.when(bv_ref[i] > 0)
    def _():
        _wait_rows(xn_hbm, xbuf.at[slot], sem.at[slot], bm)
        x = xbuf[slot].astype(BF16)
        hcat = jnp.dot(x, w1_ref[0], preferred_element_type=F32) + b1_ref[0]
        a = jnp.minimum(hcat[:, :D_FF], SWIGLU_LIMIT)
        up = jnp.clip(hcat[:, D_FF:], -SWIGLU_LIMIT, SWIGLU_LIMIT)
        glu = a * _sigmoid(SWIGLU_ALPHA * a)
        y_ref[...] = jnp.dot(((up + 1.0) * glu).astype(BF16), w2_ref[0], preferred_element_type=F32) + b2_ref[0]

    @pl.when(bv_ref[i] == 0)
    def _():
        y_ref[...] = jnp.zeros(y_ref.shape, F32)


def _ffn(block_expert, block_valid, row_tok, xn, w1, b1, w2, b2):
    nb = block_expert.shape[0]
    bm = FFN_BM
    idx3 = row_tok.reshape(nb, 1, bm)
    smem_blk = lambda fn: pl.BlockSpec((1, 1, bm), fn, memory_space=pltpu.SMEM)
    grid_spec = pltpu.PrefetchScalarGridSpec(
        num_scalar_prefetch=2,
        grid=(nb,),
        in_specs=[smem_blk(lambda i, be, bv: (0, 0, 0)),
                  smem_blk(lambda i, be, bv: (jnp.minimum(i + 1, nb - 1), 0, 0)),
                  pl.BlockSpec(memory_space=pl.ANY),
                  pl.BlockSpec((1, D_MODEL, 2 * D_FF), lambda i, be, bv: (be[i], 0, 0)),
                  pl.BlockSpec((1, 1, 2 * D_FF), lambda i, be, bv: (be[i], 0, 0)),
                  pl.BlockSpec((1, D_FF, D_MODEL), lambda i, be, bv: (be[i], 0, 0)),
                  pl.BlockSpec((1, 1, D_MODEL), lambda i, be, bv: (be[i], 0, 0))],
        out_specs=pl.BlockSpec((bm, D_MODEL), lambda i, be, bv: (i, 0)),
        scratch_shapes=[pltpu.VMEM((2, bm, D_MODEL), F32), pltpu.SemaphoreType.DMA((2,))],
    )
    return pl.pallas_call(
        functools.partial(_ffn_kernel, bm=bm, nb=nb),
        grid_spec=grid_spec,
        out_shape=jax.ShapeDtypeStruct((nb * bm, D_MODEL), F32),
        compiler_params=_cparams(("arbitrary",)),
        name="expert_ffn",
    )(block_expert, block_valid, idx3, idx3, xn, w1, b1, w2, b2)


def _combine_kernel(idx0_ref, idxn_ref, h_ref, route_ref, fn_ref, ys_hbm, o_ref, ybuf, sem, *, tm, nt):
    i = pl.program_id(0)
    slot = i % 2
    rows = TOP_K * tm

    @pl.when(i == 0)
    def _():
        _gather_rows(idx0_ref, ys_hbm, ybuf.at[0], sem.at[0], rows)

    @pl.when(i + 1 < nt)
    def _():
        _gather_rows(idxn_ref, ys_hbm, ybuf.at[1 - slot], sem.at[1 - slot], rows)

    _wait_rows(ys_hbm, ybuf.at[slot], sem.at[slot], rows)
    acc = h_ref[...]
    for k in range(TOP_K):
        acc = acc + route_ref[:, TOP_K + k:TOP_K + k + 1] * ybuf[slot, k * tm:(k + 1) * tm, :]
    o_ref[...] = _rms(acc, fn_ref[...])


def _combine(dest_t, h, route, fn, ys):
    n = h.shape[0]
    tm = COMB_TM
    nt = n // tm
    rows = TOP_K * tm
    smem_blk = lambda fn_: pl.BlockSpec((1, 1, rows), fn_, memory_space=pltpu.SMEM)
    return pl.pallas_call(
        functools.partial(_combine_kernel, tm=tm, nt=nt),
        grid=(nt,),
        in_specs=[smem_blk(lambda i: (0, 0, 0)), smem_blk(lambda i: (jnp.minimum(i + 1, nt - 1), 0, 0)),
                  pl.BlockSpec((tm, D_MODEL), lambda i: (i, 0)), pl.BlockSpec((tm, LANES), lambda i: (i, 0)),
                  pl.BlockSpec((1, D_MODEL), lambda i: (0, 0)), pl.BlockSpec(memory_space=pl.ANY)],
        out_specs=pl.BlockSpec((tm, D_MODEL), lambda i: (i, 0)),
        out_shape=jax.ShapeDtypeStruct((n, D_MODEL), F32),
        scratch_shapes=[pltpu.VMEM((2, rows, D_MODEL), F32), pltpu.SemaphoreType.DMA((2,))],
        compiler_params=_cparams(("arbitrary",)),
        name="combine",
    )(dest_t, dest_t, h, route, fn, ys)


def _rope_tables(positions, rot, lead, period):
    inv_freq = ROPE_THETA ** (-jnp.arange(0, rot, 2, dtype=F32) / rot)
    ang = positions.astype(F32)[..., None] * inv_freq
    cos, sin = jnp.cos(ang), jnp.sin(ang)
    shape = cos.shape[:-1]
    tail = period - lead - rot
    c = jnp.concatenate([jnp.ones(shape + (lead,), F32), cos, cos, jnp.ones(shape + (tail,), F32)], axis=-1)
    s = jnp.concatenate([jnp.zeros(shape + (lead,), F32), -sin, sin, jnp.zeros(shape + (tail,), F32)], axis=-1)
    reps = LANES // period
    return jnp.tile(c, reps), jnp.tile(s, reps)


def _layout_w_in(w):
    d = w.shape[0]
    sizes = (MLA_Q_RANK, MLA_KV_RANK, MLA_ROPE_DIM, NSA_HEADS * NSA_HEAD_DIM) + (NSA_KV_HEADS * NSA_HEAD_DIM,) * 6 \
        + (3 * NSA_HEADS,)
    offs = [0]
    for sz in sizes:
        offs.append(offs[-1] + sz)
    seg = [w[:, offs[j]:offs[j + 1]] for j in range(len(sizes))]
    z = lambda n: jnp.zeros((d, n), w.dtype)
    kpe = jnp.concatenate([z(MLA_NOPE_DIM), seg[2], z(LANES - MLA_NOPE_DIM - MLA_ROPE_DIM)], axis=1)
    gates = jnp.concatenate([seg[10], z(LANES - 3 * NSA_HEADS)], axis=1)
    out = jnp.concatenate([seg[0], seg[1], kpe, seg[3]] + seg[4:10] + [gates], axis=1)
    assert out.shape[1] == C_TOTAL
    return out


def _layout_mla_up(w_q_up, w_kv_up):
    rq = w_q_up.shape[0]
    qd = MLA_NOPE_DIM + MLA_ROPE_DIM
    wq = w_q_up.reshape(rq, MLA_HEADS, qd)
    wq = jnp.pad(wq, ((0, 0), (0, 0), (0, LANES - qd))).reshape(rq, MLA_HEADS * LANES)
    rk = w_kv_up.shape[0]
    wkv = w_kv_up.reshape(rk, MLA_HEADS, MLA_NOPE_DIM + MLA_V_DIM)
    wk = jnp.pad(wkv[:, :, :MLA_NOPE_DIM], ((0, 0), (0, 0), (0, LANES - MLA_NOPE_DIM))).reshape(rk, MLA_HEADS * LANES)
    wv = wkv[:, :, MLA_NOPE_DIM:].reshape(rk, MLA_HEADS * MLA_V_DIM)
    return wq, wk, wv


def _layout_compress(cmp_pos, cmp_w1, cmp_w2):
    g = NSA_KV_HEADS
    eye = jnp.eye(g, dtype=F32)
    half = CMP_BLOCK // 2
    assert CMP_BLOCK == 2 * CMP_STRIDE

    def w1_part(w):
        return jnp.einsum('ldh,gk->lgdkh', w, eye).reshape(half * g * NSA_HEAD_DIM, g * CMP_HIDDEN)

    def pos_part(p):
        return jnp.broadcast_to(p[:, None, :], (half, g, NSA_HEAD_DIM)).reshape(1, -1)

    w1a = jnp.stack([w1_part(cmp_w1[i, :half]) for i in range(2)]).astype(BF16)
    w1b = jnp.stack([w1_part(cmp_w1[i, half:]) for i in range(2)]).astype(BF16)
    w2 = jnp.stack([jnp.einsum('hd,gk->ghkd', cmp_w2[i], eye).reshape(g * CMP_HIDDEN, g * NSA_HEAD_DIM)
                    for i in range(2)]).astype(BF16)
    pos = jnp.stack([jnp.concatenate([pos_part(cmp_pos[i, :half]), pos_part(cmp_pos[i, half:])], axis=0)
                     for i in range(2)])
    return pos, w1a, w1b, w2


def _overlap_matrix(rows, n_cmp):
    n = jnp.arange(rows)[:, None]
    j = jnp.arange(LANES)[None, :] % HALF
    start = n * CMP_STRIDE
    ov = (start < (j + 1) * SEL_BLOCK) & (start + CMP_BLOCK > j * SEL_BLOCK) & (n < n_cmp)
    return ov.astype(BF16)


def _gate_expand_matrix():
    width = NSA_HEADS * NSA_HEAD_DIM
    rows = jnp.arange(LANES)[:, None]
    cols = jnp.arange(3 * width)[None, :]
    branch, head = cols // width, (cols % width) // NSA_HEAD_DIM
    return (rows == head * 3 + branch).astype(BF16)


def kernel(x, positions, attn_norm, w_in, mla_q_norm, mla_w_q_up, mla_kv_norm, mla_w_kv_up, nsa_cmp_pos,
           nsa_cmp_w1, nsa_cmp_w2, w_out, ffn_norm, router_w, router_b, moe_w1, moe_b1, moe_w2, moe_b2, final_norm):
    b, s, d = x.shape
    n = b * s
    assert attn_norm.shape[0] == 1, "single-layer configuration"
    assert d == D_MODEL and s % SEL_BLOCK == 0 and s // SEL_BLOCK <= HALF and s % ATT_T == 0
    assert s % PROJ_TM == 0 and ATT_T % PROJ_TM == 0 and n % POST_TM == 0 and n % COMB_TM == 0

    mla_c, mla_s = _rope_tables(positions.reshape(n), MLA_ROPE_DIM, MLA_NOPE_DIM, LANES)
    nsa_c, nsa_s = _rope_tables(positions.reshape(n), NSA_ROPE_DIM, 0, NSA_HEAD_DIM)
    rows = s // CMP_STRIDE
    n_cmp = (s - CMP_BLOCK) // CMP_STRIDE + 1
    cmp_end = jnp.minimum(jnp.arange(rows) * CMP_STRIDE + CMP_BLOCK - 1, s - 1)
    cmp_c, cmp_s = _rope_tables(positions[:, cmp_end], NSA_ROPE_DIM, 0, NSA_HEAD_DIM)
    ov = _overlap_matrix(rows, n_cmp)
    eg = _gate_expand_matrix()

    x2 = x.reshape(n, d)
    win = _layout_w_in(w_in[0]).astype(BF16)
    wq, wk, wv = (w.astype(BF16) for w in _layout_mla_up(mla_w_q_up[0], mla_w_kv_up[0]))
    (qm, km, vmt, qs, kc_in, vc_in, ksx, vst, kwx, vwt, gl) = _proj(
        x2, (mla_c, mla_s, nsa_c, nsa_s), attn_norm[0][None], win, mla_q_norm[0][None], wq,
        mla_kv_norm[0][None], wk, wv, b, s)

    pos, w1a, w1b, w2c = _layout_compress(nsa_cmp_pos[0], nsa_cmp_w1[0], nsa_cmp_w2[0])
    flat = CMP_STRIDE * NSA_KV_HEADS * NSA_HEAD_DIM
    kcx, vcx = _compress(kc_in.reshape(b, rows, flat), vc_in.reshape(b, rows, flat), pos, w1a, w1b, w2c,
                         cmp_c, cmp_s)

    o_mla = _mla_attention(qm.reshape(b, s, -1), km.reshape(b, s, -1), vmt)
    o_cmp, qa = _cmp_attention(qs.reshape(b, s, -1), kcx, vcx, ov)
    o_sel, o_win = _selwin_attention(qa, ksx.reshape(b, s, -1), vst, kwx.reshape(b, s, -1), vwt)

    rw = jnp.pad(router_w[0], ((0, 0), (0, LANES - N_EXPERTS)))
    rw_hi = rw.astype(BF16)
    rw_lo = (rw - rw_hi.astype(F32)).astype(BF16)
    rb = jnp.pad(router_b[0], (0, LANES - N_EXPERTS))[None]
    hres, xn, route, counts = _post(
        x2, o_mla.reshape(n, -1), o_cmp.reshape(n, -1), o_sel.reshape(n, -1), o_win.reshape(n, -1), gl, eg,
        w_out[0].astype(BF16), ffn_norm[0][None], rw_hi, rw_lo, rb)

    top_idx = route[:, 0:TOP_K].astype(I32)
    rank = route[:, 2 * TOP_K:3 * TOP_K].astype(I32)
    cnt = counts[0, :N_EXPERTS].astype(I32)
    padded = ((cnt + FFN_BM - 1) // FFN_BM) * FFN_BM
    pend = jnp.cumsum(padded)
    pstart = pend - padded
    dest = pstart[top_idx] + rank
    nb = -(-(n * TOP_K) // FFN_BM) + N_EXPERTS
    block_row = jnp.arange(nb, dtype=I32) * FFN_BM
    block_expert = jnp.minimum(jnp.sum((block_row[:, None] >= pend[None, :]).astype(I32), axis=1), N_EXPERTS - 1)
    block_valid = (block_row < pend[-1]).astype(I32)
    tok_ids = jnp.broadcast_to(jnp.arange(n, dtype=I32)[:, None], (n, TOP_K))
    row_tok = jnp.zeros((nb * FFN_BM,), I32).at[dest.reshape(-1)].set(tok_ids.reshape(-1), unique_indices=True)

    ys = _ffn(block_expert, block_valid, row_tok, xn, moe_w1[0].astype(BF16), moe_b1[0][:, None, :],
              moe_w2[0].astype(BF16), moe_b2[0][:, None, :])
    dest_t = dest.reshape(n // COMB_TM, COMB_TM, TOP_K).transpose(0, 2, 1).reshape(n // COMB_TM, 1, TOP_K * COMB_TM)
    out = _combine(dest_t, hres, route, final_norm[None], ys)
    return out.reshape(b, s, d)
```

```python
import functools
import math

import jax
import jax.numpy as jnp
from jax import lax
from jax.experimental import pallas as pl
from jax.experimental.pallas import tpu as pltpu

F32 = jnp.float32
BF16 = jnp.bfloat16
I32 = jnp.int32

D_MODEL = 1024
ROPE_THETA = 500000.0
NORM_EPS = 1e-5
NEG_INF = -1e30
POS_INF = 1e30
LOG2E = math.log2(math.e)

MLA_HEADS = 8
MLA_NOPE_DIM = 64
MLA_ROPE_DIM = 32
MLA_V_DIM = 64
MLA_Q_RANK = 256
MLA_KV_RANK = 128

NSA_HEADS = 8
NSA_KV_HEADS = 2
NSA_GROUP = NSA_HEADS // NSA_KV_HEADS
NSA_HEAD_DIM = 64
NSA_ROPE_DIM = NSA_HEAD_DIM // 4
CMP_BLOCK = 32
CMP_STRIDE = 16
CMP_HIDDEN = 2 * NSA_HEAD_DIM
SEL_BLOCK = 64
N_SEL = 16
N_LOCAL_SEL = 2
WINDOW = 512

N_EXPERTS = 32
TOP_K = 4
D_FF = D_MODEL
SWIGLU_LIMIT = 7.0
SWIGLU_ALPHA = 1.702

LANES = 128
HALF = LANES // 2
VMEM_LIMIT = 48 * 1024 * 1024

PROJ_TM = 256
ATT_T = 512
ATT_QS = 256
ATT_KC_FULL = 512
ATT_KC_DIAG = 256
ATT_AHEAD = 3
CMP_TQ = 256
POST_TM = 256
FFN_BM = 512
COMB_TM = 128
DMA_UNROLL = 8
ROW_CHUNKS = D_MODEL // LANES

C_QLAT = 0
C_KVLAT = C_QLAT + MLA_Q_RANK
C_KPE = C_KVLAT + MLA_KV_RANK
C_QNSA = C_KPE + LANES
C_KC = C_QNSA + NSA_HEADS * NSA_HEAD_DIM
C_VC = C_KC + LANES
C_KS = C_VC + LANES
C_VS = C_KS + LANES
C_KW = C_VS + LANES
C_VW = C_KW + LANES
C_GATE = C_VW + LANES
C_TOTAL = C_GATE + LANES


def _cparams(sem):
    return pltpu.CompilerParams(dimension_semantics=sem, vmem_limit_bytes=VMEM_LIMIT)


def _rms(x, g):
    return x * lax.rsqrt(jnp.mean(x * x, axis=-1, keepdims=True) + NORM_EPS) * g


def _sigmoid(x):
    return 1.0 / (1.0 + jnp.exp(-x))


def _rope(x, c, s, half, first):
    n = x.shape[-1]
    partner = jnp.where(first, pltpu.roll(x, n - half, 1), pltpu.roll(x, half, 1))
    return x * c + partner * s


def _expand_pair(x, lo, fill):
    y = pltpu.roll(x, HALF, 1)
    return (jnp.where(lo, x, fill), jnp.where(lo, fill, y), jnp.where(lo, y, fill), jnp.where(lo, fill, x))


def _proj_kernel(x_ref, mc_ref, ms_ref, nc_ref, ns_ref, an_ref, win_ref, qn_ref, wq_ref, kvn_ref, wk_ref,
                 wv_ref, qm_ref, km_ref, vm_ref, qs_ref, kc_ref, vc_ref, ks_ref, vs_ref, kw_ref, vw_ref,
                 g_ref, *, tm, seq):
    x = x_ref[...]
    u = _rms(x, an_ref[...])
    y = jnp.dot(u.astype(BF16), win_ref[...], preferred_element_type=F32)
    lane = lax.broadcasted_iota(I32, (1, LANES), 1)
    lo = lane < HALF
    mla_first = lane < MLA_NOPE_DIM + MLA_ROPE_DIM // 2
    nsa_first = (lane % NSA_HEAD_DIM) < NSA_ROPE_DIM // 2
    mc, ms, nc, ns = mc_ref[...], ms_ref[...], nc_ref[...], ns_ref[...]

    qn = _rms(y[:, C_QLAT:C_QLAT + MLA_Q_RANK], qn_ref[...])
    q = jnp.dot(qn.astype(BF16), wq_ref[...], preferred_element_type=F32)
    q = q * ((MLA_NOPE_DIM + MLA_ROPE_DIM) ** -0.5 * LOG2E)
    for h in range(MLA_HEADS):
        sl = slice(h * LANES, (h + 1) * LANES)
        qm_ref[:, sl] = _rope(q[:, sl], mc, ms, MLA_ROPE_DIM // 2, mla_first).astype(BF16)
    kvn = _rms(y[:, C_KVLAT:C_KVLAT + MLA_KV_RANK], kvn_ref[...]).astype(BF16)
    kpe = _rope(y[:, C_KPE:C_KPE + LANES], mc, ms, MLA_ROPE_DIM // 2, mla_first)
    kn = jnp.dot(kvn, wk_ref[...], preferred_element_type=F32)
    for h in range(MLA_HEADS):
        sl = slice(h * LANES, (h + 1) * LANES)
        km_ref[:, sl] = (kn[:, sl] + kpe).astype(BF16)
    vm_ref[0, 0] = jnp.dot(kvn, wv_ref[...], preferred_element_type=F32).T.astype(BF16)

    for c in range(NSA_HEADS * NSA_HEAD_DIM // LANES):
        ch = y[:, C_QNSA + c * LANES:C_QNSA + (c + 1) * LANES]
        qs_ref[:, c * LANES:(c + 1) * LANES] = (
            _rope(ch, nc, ns, NSA_ROPE_DIM // 2, nsa_first) * (NSA_HEAD_DIM ** -0.5 * LOG2E)).astype(BF16)
    kc_ref[...] = y[:, C_KC:C_KC + LANES]
    vc_ref[...] = y[:, C_VC:C_VC + LANES]
    row = lax.broadcasted_iota(I32, (tm, LANES), 0)
    tok = (pl.program_id(0) * tm + row) % seq
    onehot = ((lane % HALF) == tok // SEL_BLOCK).astype(F32)
    ks = _rope(y[:, C_KS:C_KS + LANES], nc, ns, NSA_ROPE_DIM // 2, nsa_first)
    kw = _rope(y[:, C_KW:C_KW + LANES], nc, ns, NSA_ROPE_DIM // 2, nsa_first)
    for ref, val, fill in ((ks_ref, ks, onehot), (kw_ref, kw, 0.0)):
        for c, chunk in enumerate(_expand_pair(val, lo, fill)):
            ref[:, c * LANES:(c + 1) * LANES] = chunk.astype(BF16)
    vs_ref[0, 0] = y[:, C_VS:C_VS + LANES].T.astype(BF16)
    vw_ref[0, 0] = y[:, C_VW:C_VW + LANES].T.astype(BF16)
    g_ref[...] = y[:, C_GATE:C_GATE + LANES]


def _proj(x2, tabs, an, win, qn, wq, kvn, wk, wv, batch, seq):
    n = x2.shape[0]
    tm = PROJ_TM
    per_b, per_slab = seq // tm, ATT_T // tm
    tok = lambda w: pl.BlockSpec((tm, w), lambda i: (i, 0))
    tr = lambda r: pl.BlockSpec((1, 1, r, tm),
                                lambda i: (i // per_b, (i % per_b) // per_slab, 0, (i % per_b) % per_slab))
    full = lambda a: pl.BlockSpec(a.shape, lambda i: (0,) * a.ndim)
    tok_out = lambda w, d: (tok(w), jax.ShapeDtypeStruct((n, w), d))
    tr_out = lambda r: (tr(r), jax.ShapeDtypeStruct((batch, seq // ATT_T, r, ATT_T), BF16))
    outs = [tok_out(8 * LANES, BF16), tok_out(8 * LANES, BF16), tr_out(MLA_HEADS * MLA_V_DIM),
            tok_out(4 * LANES, BF16), tok_out(LANES, F32), tok_out(LANES, F32),
            tok_out(4 * LANES, BF16), tr_out(LANES), tok_out(4 * LANES, BF16), tr_out(LANES), tok_out(LANES, F32)]
    return pl.pallas_call(
        functools.partial(_proj_kernel, tm=tm, seq=seq),
        grid=(n // tm,),
        in_specs=[tok(D_MODEL)] + [tok(LANES)] * 4 + [full(a) for a in (an, win, qn, wq, kvn, wk, wv)],
        out_specs=[o[0] for o in outs],
        out_shape=[o[1] for o in outs],
        compiler_params=_cparams(("parallel",)),
        name="proj",
    )(x2, *tabs, an, win, qn, wq, kvn, wk, wv)


def _compress_kernel(kin_ref, vin_ref, pos_ref, w1a_ref, w1b_ref, w2_ref, cc_ref, cs_ref, kc_ref, vc_ref, *, rows):
    lane = lax.broadcasted_iota(I32, (1, LANES), 1)
    lo = lane < HALF
    nsa_first = (lane % NSA_HEAD_DIM) < NSA_ROPE_DIM // 2

    def comp(x, i):
        p = jnp.dot((x + pos_ref[i, 0:1, :]).astype(BF16), w1a_ref[i], preferred_element_type=F32)
        q = jnp.dot((x + pos_ref[i, 1:2, :]).astype(BF16), w1b_ref[i], preferred_element_type=F32)
        hid = p + pltpu.roll(q, rows - 1, 0)
        hid = hid * _sigmoid(hid)
        return jnp.dot(hid.astype(BF16), w2_ref[i], preferred_element_type=F32)

    kc = _rope(comp(kin_ref[0], 0), cc_ref[0], cs_ref[0], NSA_ROPE_DIM // 2, nsa_first)
    vc = comp(vin_ref[0], 1)
    for ref, val in ((kc_ref, kc), (vc_ref, vc)):
        for c, chunk in enumerate(_expand_pair(val, lo, 0.0)):
            ref[0, :, c * LANES:(c + 1) * LANES] = chunk.astype(BF16)


def _compress(kin, vin, pos, w1a, w1b, w2, cc, cs):
    b, rows, width = kin.shape
    per_b = lambda w: pl.BlockSpec((1, rows, w), lambda i: (i, 0, 0))
    full = lambda a: pl.BlockSpec(a.shape, lambda i: (0,) * a.ndim)
    return pl.pallas_call(
        functools.partial(_compress_kernel, rows=rows),
        grid=(b,),
        in_specs=[per_b(width), per_b(width), full(pos), full(w1a), full(w1b), full(w2), per_b(LANES), per_b(LANES)],
        out_specs=[per_b(4 * LANES), per_b(4 * LANES)],
        out_shape=[jax.ShapeDtypeStruct((b, rows, 4 * LANES), BF16)] * 2,
        compiler_params=_cparams(("parallel",)),
        name="compress",
    )(kin, vin, pos, w1a, w1b, w2, cc, cs)


def _subtile_plan(t, kc, delta, window):
    plan = []
    for j in range(t // ATT_QS):
        for c in range(t // kc):
            if delta is None:
                plan.append((j, c, False))
                continue
            dmin = delta + j * ATT_QS - (c * kc + kc - 1)
            dmax = delta + j * ATT_QS + ATT_QS - 1 - c * kc
            if dmax < 0 or (window is not None and dmin >= window):
                continue
            full = dmin >= 0 and (window is None or dmax < window)
            plan.append((j, c, not full))
    return plan


def _flash_tile(q_of, k_of, vt_of, m_ref, l_ref, acc_ref, heads, t, kc, delta, window):
    plan = _subtile_plan(t, kc, delta, window)
    chains = [(h, j) for h in heads for j in sorted({jj for jj, _, _ in plan})]
    jsl = lambda j: slice(j * ATT_QS, (j + 1) * ATT_QS)
    state = {(h, j): (m_ref[h:h + 1, jsl(j)], l_ref[h:h + 1, jsl(j)], acc_ref[h, :, jsl(j)]) for h, j in chains}
    items = [(h, j, c, masked) for h, j in chains for jj, c, masked in plan if jj == j]

    def scores(item):
        h, j, c, masked = item
        st = lax.dot_general(k_of(h, c, kc), q_of(h, j), (((1,), (1,)), ((), ())), preferred_element_type=F32)
        if masked:
            kk = lax.broadcasted_iota(I32, (kc, ATT_QS), 0)
            qq = lax.broadcasted_iota(I32, (kc, ATT_QS), 1)
            d = (delta + j * ATT_QS - c * kc) + qq - kk
            ok = d >= 0
            if window is not None:
                ok = ok & (d < window)
            st = jnp.where(ok, st, NEG_INF)
        return st

    ahead = [scores(it) for it in items[:ATT_AHEAD]]
    pending = None
    for n, (h, j, c, _) in enumerate(items):
        st = ahead.pop(0)
        if n + ATT_AHEAD < len(items):
            ahead.append(scores(items[n + ATT_AHEAD]))
        m, l, acc = state[(h, j)]
        m_new = jnp.maximum(m, jnp.max(st, axis=0, keepdims=True))
        alpha = jnp.exp2(m - m_new)
        p = jnp.exp2(st - m_new)
        l = alpha * l + jnp.sum(p, axis=0, keepdims=True)
        pv = jnp.dot(vt_of(h, c, kc), p.astype(BF16), preferred_element_type=F32)
        if pending is not None:
            key, a_prev, pv_prev = pending
            mm, ll, aa = state[key]
            state[key] = (mm, ll, aa * a_prev + pv_prev)
            if key == (h, j):
                acc = state[key][2]
        state[(h, j)] = (m_new, l, acc)
        pending = ((h, j), alpha, pv)
    key, a_prev, pv_prev = pending
    mm, ll, aa = state[key]
    state[key] = (mm, ll, aa * a_prev + pv_prev)
    for h, j in chains:
        m, l, acc = state[(h, j)]
        m_ref[h:h + 1, jsl(j)] = m
        l_ref[h:h + 1, jsl(j)] = l
        acc_ref[h, :, jsl(j)] = acc


def _flash_init(m_ref, l_ref, acc_ref):
    m_ref[...] = jnp.full(m_ref.shape, NEG_INF, F32)
    l_ref[...] = jnp.zeros(l_ref.shape, F32)
    acc_ref[...] = jnp.zeros(acc_ref.shape, F32)


def _flash_out_pair(l_ref, acc_ref, pair):
    parts = [acc_ref[h] / l_ref[h:h + 1, :] for h in (2 * pair, 2 * pair + 1)]
    return jnp.concatenate(parts, axis=0).T


def _kv_readers(k_ref, vt_ref, kt, t, k_chunk_of_head, v_rows_of_head):
    def k_of(h, c, kc):
        start = pl.multiple_of(kt * t + c * kc, kc)
        return k_ref[0, pl.ds(start, kc), k_chunk_of_head(h)]

    def vt_of(h, c, kc):
        return vt_ref[0, kt, v_rows_of_head(h), c * kc:(c + 1) * kc]

    return k_of, vt_of


def _mla_kernel(q_ref, k_ref, vt_ref, o_ref, m_ref, l_ref, acc_ref, *, t):
    qi = pl.program_id(2)
    heads = (0, 1)
    chunk = lambda h: slice(h * LANES, (h + 1) * LANES)
    vrows = lambda h: slice(h * MLA_V_DIM, (h + 1) * MLA_V_DIM)
    q_of = lambda h, j: q_ref[0, j * ATT_QS:(j + 1) * ATT_QS, chunk(h)]
    _flash_init(m_ref, l_ref, acc_ref)

    def tile(kt, kc, delta):
        k_of, vt_of = _kv_readers(k_ref, vt_ref, kt, t, chunk, vrows)
        _flash_tile(q_of, k_of, vt_of, m_ref, l_ref, acc_ref, heads, t, kc, delta, None)

    def body(kt, carry):
        tile(kt, ATT_KC_FULL, None)
        return carry

    lax.fori_loop(0, qi, body, 0)
    tile(qi, ATT_KC_DIAG, 0)
    o_ref[0] = _flash_out_pair(l_ref, acc_ref, 0)


def _mla_attention(q, k, vt):
    b, s, _ = q.shape
    t = ATT_T
    pairs = MLA_HEADS // 2
    return pl.pallas_call(
        functools.partial(_mla_kernel, t=t),
        grid=(b, pairs, s // t),
        in_specs=[pl.BlockSpec((1, t, 2 * LANES), lambda bi, p, qi: (bi, qi, p)),
                  pl.BlockSpec((1, s, 2 * LANES), lambda bi, p, qi: (bi, 0, p)),
                  pl.BlockSpec((1, s // t, 2 * MLA_V_DIM, t), lambda bi, p, qi: (bi, 0, p, 0))],
        out_specs=pl.BlockSpec((1, t, LANES), lambda bi, p, qi: (bi, qi, p)),
        out_shape=jax.ShapeDtypeStruct((b, s, pairs * LANES), F32),
        scratch_shapes=[pltpu.VMEM((2, t), F32), pltpu.VMEM((2, t), F32), pltpu.VMEM((2, MLA_V_DIM, t), F32)],
        compiler_params=_cparams(("parallel", "parallel", "arbitrary")),
        name="mla_attention",
    )(q, k, vt)


def _cmp_kernel(q_ref, kc_ref, vc_ref, ov_ref, o_ref, qa_ref, *, tq, rows):
    qi = pl.program_id(2)
    lane = lax.broadcasted_iota(I32, (1, LANES), 1)
    lo = lane < HALF
    t_pos = qi * tq + lax.broadcasted_iota(I32, (tq, 1), 0)
    cmp_end = lax.broadcasted_iota(I32, (1, rows), 1) * CMP_STRIDE + (CMP_BLOCK - 1)
    mask = cmp_end <= t_pos
    live = (t_pos >= CMP_BLOCK - 1).astype(F32)

    psum = jnp.zeros((tq, rows), F32)
    for pair in range(NSA_GROUP // 2):
        qp = q_ref[0, :, pair * LANES:(pair + 1) * LANES]
        acc = jnp.zeros((tq, LANES), F32)
        for e in range(2):
            sl = slice(e * LANES, (e + 1) * LANES)
            s = lax.dot_general(qp, kc_ref[0, :, sl], (((1,), (1,)), ((), ())), preferred_element_type=F32)
            s = jnp.where(mask, s, NEG_INF)
            p = jnp.exp2(s - jnp.max(s, axis=-1, keepdims=True))
            p = p / jnp.sum(p, axis=-1, keepdims=True) * live
            psum = psum + p
            acc = acc + jnp.dot(p.astype(BF16), vc_ref[0, :, sl], preferred_element_type=F32)
        o_ref[0, :, pair * LANES:(pair + 1) * LANES] = acc

    p_hi = psum.astype(BF16)
    p_lo = (psum - p_hi.astype(F32)).astype(BF16)
    imp = (jnp.dot(p_hi, ov_ref[...], preferred_element_type=F32)
           + jnp.dot(p_lo, ov_ref[...], preferred_element_type=F32))
    blk = lane % HALF
    cur = t_pos // SEL_BLOCK
    forced = (blk == 0) | ((blk <= cur) & (blk > cur - N_LOCAL_SEL))
    val = jnp.where(forced, POS_INF, jnp.where(blk <= cur, imp, NEG_INF))

    vt = val.T[:HALF]
    sub = 8
    groups = [vt[a * sub:(a + 1) * sub] for a in range(HALF // sub)]
    cnts = [jnp.zeros((sub, tq), I32) for _ in groups]
    jsub = lax.broadcasted_iota(I32, (sub, tq), 0)
    for i in range(HALF):
        r = vt[i:i + 1, :]
        for a, grp in enumerate(groups):
            if a > i // sub:
                ahead = r >= grp
            elif a < i // sub:
                ahead = r > grp
            else:
                ahead = (r > grp) | ((r == grp) & (i % sub < jsub))
            cnts[a] = cnts[a] + ahead.astype(I32)
    cnt = jnp.concatenate(cnts, axis=0)
    bias_t = jnp.where(cnt < N_SEL, 0.0, NEG_INF).astype(F32)
    bias = jnp.concatenate([bias_t, bias_t], axis=0).T.astype(BF16)
    for h in range(NSA_GROUP):
        qp = q_ref[0, :, (h // 2) * LANES:(h // 2 + 1) * LANES]
        chunk = jnp.where(lo, qp, bias) if h % 2 == 0 else jnp.where(lo, bias, qp)
        qa_ref[0, :, h * LANES:(h + 1) * LANES] = chunk


def _cmp_attention(q, kcx, vcx, ov):
    b, s, _ = q.shape
    rows = kcx.shape[1]
    tq = min(CMP_TQ, s)
    g = NSA_KV_HEADS
    return pl.pallas_call(
        functools.partial(_cmp_kernel, tq=tq, rows=rows),
        grid=(b, g, s // tq),
        in_specs=[pl.BlockSpec((1, tq, 2 * LANES), lambda bi, gi, qi: (bi, qi, gi)),
                  pl.BlockSpec((1, rows, 2 * LANES), lambda bi, gi, qi: (bi, 0, gi)),
                  pl.BlockSpec((1, rows, 2 * LANES), lambda bi, gi, qi: (bi, 0, gi)),
                  pl.BlockSpec(ov.shape, lambda bi, gi, qi: (0, 0))],
        out_specs=[pl.BlockSpec((1, tq, 2 * LANES), lambda bi, gi, qi: (bi, qi, gi)),
                   pl.BlockSpec((1, tq, 4 * LANES), lambda bi, gi, qi: (bi, qi, gi))],
        out_shape=[jax.ShapeDtypeStruct((b, s, g * 2 * LANES), F32),
                   jax.ShapeDtypeStruct((b, s, g * 4 * LANES), BF16)],
        compiler_params=_cparams(("parallel", "parallel", "parallel")),
        name="cmp_attention",
    )(q, kcx, vcx, ov)


def _selwin_kernel(qa_ref, ks_ref, vs_ref, kw_ref, vw_ref, os_ref, ow_ref,
                   ms_ref, ls_ref, as_ref, mw_ref, lw_ref, aw_ref, *, t):
    qi = pl.program_id(2)
    heads = tuple(range(NSA_GROUP))
    chunk = lambda h: slice((h % 2) * LANES, (h % 2 + 1) * LANES)
    vrows = lambda h: slice(0, NSA_HEAD_DIM)
    q_of = lambda h, j: qa_ref[0, j * ATT_QS:(j + 1) * ATT_QS, h * LANES:(h + 1) * LANES]
    _flash_init(ms_ref, ls_ref, as_ref)
    _flash_init(mw_ref, lw_ref, aw_ref)

    def sel_tile(kt, kc, delta):
        k_of, vt_of = _kv_readers(ks_ref, vs_ref, kt, t, chunk, vrows)
        _flash_tile(q_of, k_of, vt_of, ms_ref, ls_ref, as_ref, heads, t, kc, delta, None)

    def win_tile(kt, delta):
        k_of, vt_of = _kv_readers(kw_ref, vw_ref, kt, t, chunk, vrows)
        _flash_tile(q_of, k_of, vt_of, mw_ref, lw_ref, aw_ref, heads, t, ATT_KC_DIAG, delta, WINDOW)

    def body(kt, carry):
        sel_tile(kt, ATT_KC_FULL, None)
        return carry

    lax.fori_loop(0, qi, body, 0)

    @pl.when(qi > 0)
    def _():
        win_tile(qi - 1, t)

    sel_tile(qi, ATT_KC_DIAG, 0)
    win_tile(qi, 0)
    for pair in range(NSA_GROUP // 2):
        sl = slice(pair * LANES, (pair + 1) * LANES)
        os_ref[0, :, sl] = _flash_out_pair(ls_ref, as_ref, pair)
        ow_ref[0, :, sl] = _flash_out_pair(lw_ref, aw_ref, pair)


def _selwin_attention(qa, ksx, vst, kwx, vwt):
    b, s, _ = qa.shape
    t = ATT_T
    assert t >= WINDOW, "window branch reads only the previous and the diagonal key tile"
    g = NSA_KV_HEADS
    pairs = NSA_GROUP // 2
    kspec = pl.BlockSpec((1, s, 2 * LANES), lambda bi, gi, qi: (bi, 0, gi))
    vspec = pl.BlockSpec((1, s // t, NSA_HEAD_DIM, t), lambda bi, gi, qi: (bi, 0, gi, 0))
    ospec = pl.BlockSpec((1, t, pairs * LANES), lambda bi, gi, qi: (bi, qi, gi))
    stat = lambda: pltpu.VMEM((NSA_GROUP, t), F32)
    accs = lambda: pltpu.VMEM((NSA_GROUP, NSA_HEAD_DIM, t), F32)
    return pl.pallas_call(
        functools.partial(_selwin_kernel, t=t),
        grid=(b, g, s // t),
        in_specs=[pl.BlockSpec((1, t, NSA_GROUP * LANES), lambda bi, gi, qi: (bi, qi, gi)),
                  kspec, vspec, kspec, vspec],
        out_specs=[ospec, ospec],
        out_shape=[jax.ShapeDtypeStruct((b, s, g * pairs * LANES), F32)] * 2,
        scratch_shapes=[stat(), stat(), accs(), stat(), stat(), accs()],
        compiler_params=_cparams(("parallel", "parallel", "arbitrary")),
        name="selwin_attention",
    )(qa, ksx, vst, kwx, vwt)


def _post_kernel(x_ref, om_ref, oc_ref, os_ref, ow_ref, g_ref, eg_ref, wo_ref, fn_ref, rwh_ref, rwl_ref, rb_ref,
                 h_ref, xn_ref, route_ref, cnt_ref, *, tm):
    i = pl.program_id(0)
    width = NSA_HEADS * NSA_HEAD_DIM
    sg = _sigmoid(g_ref[...])
    sg_hi = sg.astype(BF16)
    sg_lo = (sg - sg_hi.astype(F32)).astype(BF16)
    ge = (jnp.dot(sg_hi, eg_ref[...], preferred_element_type=F32)
          + jnp.dot(sg_lo, eg_ref[...], preferred_element_type=F32))
    o_nsa = (ge[:, 0:width] * oc_ref[...] + ge[:, width:2 * width] * os_ref[...]
             + ge[:, 2 * width:3 * width] * ow_ref[...])
    mla_w = MLA_HEADS * MLA_V_DIM
    mixed = (jnp.dot(om_ref[...].astype(BF16), wo_ref[0:mla_w, :], preferred_element_type=F32)
             + jnp.dot(o_nsa.astype(BF16), wo_ref[mla_w:mla_w + width, :], preferred_element_type=F32))
    h = x_ref[...] + mixed
    h_ref[...] = h
    xn = _rms(h, fn_ref[...])
    _lanes_to_rows(xn_ref, xn)

    x_hi = xn.astype(BF16)
    x_lo = (xn - x_hi.astype(F32)).astype(BF16)
    logits = (jnp.dot(x_hi, rwh_ref[...], preferred_element_type=F32)
              + jnp.dot(x_hi, rwl_ref[...], preferred_element_type=F32)
              + jnp.dot(x_lo, rwh_ref[...], preferred_element_type=F32)) + rb_ref[...]
    lane = lax.broadcasted_iota(I32, (tm, LANES), 1)
    lg = jnp.where(lane < N_EXPERTS, logits, -jnp.inf)
    vals, hots = [], []
    for _ in range(TOP_K):
        m = jnp.max(lg, axis=-1, keepdims=True)
        idx = jnp.min(jnp.where(lg == m, lane, LANES), axis=-1, keepdims=True)
        hot = lane == idx
        lg = jnp.where(hot, -jnp.inf, lg)
        vals.append(m)
        hots.append(hot)
    es = [jnp.exp(v - vals[0]) for v in vals]
    den = es[0] + es[1] + es[2] + es[3]

    @pl.when(i == 0)
    def _():
        cnt_ref[...] = jnp.zeros(cnt_ref.shape, F32)

    hot_all = (hots[0] | hots[1] | hots[2] | hots[3]).astype(F32)
    r = lax.broadcasted_iota(I32, (tm, tm), 0)
    c = lax.broadcasted_iota(I32, (tm, tm), 1)
    tri = (c < r).astype(BF16)
    before = jnp.dot(tri, hot_all.astype(BF16), preferred_element_type=F32) + cnt_ref[...]
    route = jnp.zeros((tm, LANES), F32)
    for k in range(TOP_K):
        e_k = jnp.sum(jnp.where(hots[k], lane, 0), axis=-1, keepdims=True).astype(F32)
        rank_k = jnp.sum(jnp.where(hots[k], before, 0.0), axis=-1, keepdims=True)
        route = (route + jnp.where(lane == k, e_k, 0.0) + jnp.where(lane == TOP_K + k, es[k] / den, 0.0)
                 + jnp.where(lane == 2 * TOP_K + k, rank_k, 0.0))
    route_ref[...] = route
    cnt_ref[...] = cnt_ref[...] + jnp.sum(hot_all, axis=0, keepdims=True)


def _post(x2, om, oc, os_, ow, gl, eg, wo, fn, rwh, rwl, rb):
    n = x2.shape[0]
    tm = POST_TM
    tok = lambda w: pl.BlockSpec((tm, w), lambda i: (i, 0))
    full = lambda a: pl.BlockSpec(a.shape, lambda i: (0,) * a.ndim)
    return pl.pallas_call(
        functools.partial(_post_kernel, tm=tm),
        grid=(n // tm,),
        in_specs=[tok(D_MODEL), tok(om.shape[1]), tok(oc.shape[1]), tok(os_.shape[1]), tok(ow.shape[1]), tok(LANES)]
        + [full(a) for a in (eg, wo, fn, rwh, rwl, rb)],
        out_specs=[tok(D_MODEL), pl.BlockSpec((tm * ROW_CHUNKS, LANES), lambda i: (i, 0)), tok(LANES),
                   pl.BlockSpec((1, LANES), lambda i: (0, 0))],
        out_shape=[jax.ShapeDtypeStruct((n, D_MODEL), F32), jax.ShapeDtypeStruct((n * ROW_CHUNKS, LANES), F32),
                   jax.ShapeDtypeStruct((n, LANES), F32), jax.ShapeDtypeStruct((1, LANES), F32)],
        compiler_params=_cparams(("arbitrary",)),
        name="post",
    )(x2, om, oc, os_, ow, gl, eg, wo, fn, rwh, rwl, rb)


def _gather_rows(idx_ref, src_hbm, dst, sem, count, unroll):
    def body(r, carry):
        src = pl.multiple_of(idx_ref[0, 0, r] * ROW_CHUNKS, ROW_CHUNKS)
        dst_row = pl.multiple_of(r * ROW_CHUNKS, ROW_CHUNKS)
        pltpu.make_async_copy(src_hbm.at[pl.ds(src, ROW_CHUNKS), :], dst.at[pl.ds(dst_row, ROW_CHUNKS), :], sem).start()
        return carry
    lax.fori_loop(0, count, body, 0, unroll=unroll)


def _wait_rows(src_hbm, dst, sem, count):
    pltpu.make_async_copy(src_hbm.at[pl.ds(0, count * ROW_CHUNKS), :], dst, sem).wait()


def _rows_to_lanes(ref, lead, first, rows, dtype):
    return jnp.concatenate(
        [ref[lead + (pl.ds(first * ROW_CHUNKS + c, rows, stride=ROW_CHUNKS), slice(None))].astype(dtype)
         for c in range(ROW_CHUNKS)], axis=1)


def _lanes_to_rows(ref, val):
    for c in range(ROW_CHUNKS):
        ref[pl.ds(c, val.shape[0], stride=ROW_CHUNKS), :] = val[:, c * LANES:(c + 1) * LANES]


def _ffn_kernel(be_ref, bv_ref, idx0_ref, idxn_ref, xn_hbm, w1_ref, b1_ref, w2_ref, b2_ref, y_ref, xbuf, xb_ref, sem,
                *, bm):
    del be_ref
    i = pl.program_id(0)
    slot = i % 2

    @pl.when(i == 0)
    def _():
        _gather_rows(idx0_ref, xn_hbm, xbuf.at[0], sem.at[0], bm, DMA_UNROLL)

    @pl.when(bv_ref[i] > 0)
    def _():
        _wait_rows(xn_hbm, xbuf.at[slot], sem.at[slot], bm)
        xb_ref[...] = _rows_to_lanes(xbuf, (slot,), 0, bm, BF16)
        _gather_rows(idxn_ref, xn_hbm, xbuf.at[1 - slot], sem.at[1 - slot], bm, True)
        hcat = jnp.dot(xb_ref[...], w1_ref[0], preferred_element_type=F32) + b1_ref[0]
        a = jnp.minimum(hcat[:, :D_FF], SWIGLU_LIMIT)
        up = jnp.clip(hcat[:, D_FF:], -SWIGLU_LIMIT, SWIGLU_LIMIT)
        glu = a * _sigmoid(SWIGLU_ALPHA * a)
        y = jnp.dot(((up + 1.0) * glu).astype(BF16), w2_ref[0], preferred_element_type=F32) + b2_ref[0]
        _lanes_to_rows(y_ref, y)

    @pl.when(bv_ref[i] == 0)
    def _():
        @pl.when(bv_ref[jnp.maximum(i - 1, 0)] > 0)
        def _():
            _wait_rows(xn_hbm, xbuf.at[slot], sem.at[slot], bm)

        y_ref[...] = jnp.zeros(y_ref.shape, F32)


def _ffn(block_expert, block_valid, row_tok, xn, w1, b1, w2, b2):
    nb = block_expert.shape[0]
    bm = FFN_BM
    idx3 = row_tok.reshape(nb, 1, bm)
    smem_blk = lambda fn: pl.BlockSpec((1, 1, bm), fn, memory_space=pltpu.SMEM)
    grid_spec = pltpu.PrefetchScalarGridSpec(
        num_scalar_prefetch=2,
        grid=(nb,),
        in_specs=[smem_blk(lambda i, be, bv: (0, 0, 0)),
                  smem_blk(lambda i, be, bv: (jnp.minimum(i + 1, nb - 1), 0, 0)),
                  pl.BlockSpec(memory_space=pl.ANY),
                  pl.BlockSpec((1, D_MODEL, 2 * D_FF), lambda i, be, bv: (be[i], 0, 0)),
                  pl.BlockSpec((1, 1, 2 * D_FF), lambda i, be, bv: (be[i], 0, 0)),
                  pl.BlockSpec((1, D_FF, D_MODEL), lambda i, be, bv: (be[i], 0, 0)),
                  pl.BlockSpec((1, 1, D_MODEL), lambda i, be, bv: (be[i], 0, 0))],
        out_specs=pl.BlockSpec((bm * ROW_CHUNKS, LANES), lambda i, be, bv: (i, 0)),
        scratch_shapes=[pltpu.VMEM((2, bm * ROW_CHUNKS, LANES), F32), pltpu.VMEM((bm, D_MODEL), BF16),
                        pltpu.SemaphoreType.DMA((2,))],
    )
    return pl.pallas_call(
        functools.partial(_ffn_kernel, bm=bm),
        grid_spec=grid_spec,
        out_shape=jax.ShapeDtypeStruct((nb * bm * ROW_CHUNKS, LANES), F32),
        compiler_params=_cparams(("arbitrary",)),
        name="expert_ffn",
    )(block_expert, block_valid, idx3, idx3, xn, w1, b1, w2, b2)


def _combine_kernel(idx0_ref, idxn_ref, h_ref, route_ref, fn_ref, ys_hbm, o_ref, ybuf, sem, *, tm, nt):
    i = pl.program_id(0)
    slot = i % 2
    rows = TOP_K * tm

    @pl.when(i == 0)
    def _():
        _gather_rows(idx0_ref, ys_hbm, ybuf.at[0], sem.at[0], rows, DMA_UNROLL)

    @pl.when(i + 1 < nt)
    def _():
        _gather_rows(idxn_ref, ys_hbm, ybuf.at[1 - slot], sem.at[1 - slot], rows, DMA_UNROLL)

    _wait_rows(ys_hbm, ybuf.at[slot], sem.at[slot], rows)
    acc = h_ref[...]
    for k in range(TOP_K):
        acc = acc + route_ref[:, TOP_K + k:TOP_K + k + 1] * _rows_to_lanes(ybuf, (slot,), k * tm, tm, F32)
    o_ref[...] = _rms(acc, fn_ref[...])


def _combine(dest_t, h, route, fn, ys):
    n = h.shape[0]
    tm = COMB_TM
    nt = n // tm
    rows = TOP_K * tm
    smem_blk = lambda fn_: pl.BlockSpec((1, 1, rows), fn_, memory_space=pltpu.SMEM)
    return pl.pallas_call(
        functools.partial(_combine_kernel, tm=tm, nt=nt),
        grid=(nt,),
        in_specs=[smem_blk(lambda i: (0, 0, 0)), smem_blk(lambda i: (jnp.minimum(i + 1, nt - 1), 0, 0)),
                  pl.BlockSpec((tm, D_MODEL), lambda i: (i, 0)), pl.BlockSpec((tm, LANES), lambda i: (i, 0)),
                  pl.BlockSpec((1, D_MODEL), lambda i: (0, 0)), pl.BlockSpec(memory_space=pl.ANY)],
        out_specs=pl.BlockSpec((tm, D_MODEL), lambda i: (i, 0)),
        out_shape=jax.ShapeDtypeStruct((n, D_MODEL), F32),
        scratch_shapes=[pltpu.VMEM((2, rows * ROW_CHUNKS, LANES), F32), pltpu.SemaphoreType.DMA((2,))],
        compiler_params=_cparams(("arbitrary",)),
        name="combine",
    )(dest_t, dest_t, h, route, fn, ys)


def _rope_tables(positions, rot, lead, period):
    inv_freq = ROPE_THETA ** (-jnp.arange(0, rot, 2, dtype=F32) / rot)
    ang = positions.astype(F32)[..., None] * inv_freq
    cos, sin = jnp.cos(ang), jnp.sin(ang)
    shape = cos.shape[:-1]
    tail = period - lead - rot
    c = jnp.concatenate([jnp.ones(shape + (lead,), F32), cos, cos, jnp.ones(shape + (tail,), F32)], axis=-1)
    s = jnp.concatenate([jnp.zeros(shape + (lead,), F32), -sin, sin, jnp.zeros(shape + (tail,), F32)], axis=-1)
    reps = LANES // period
    return jnp.tile(c, reps), jnp.tile(s, reps)


def _layout_w_in(w):
    d = w.shape[0]
    sizes = (MLA_Q_RANK, MLA_KV_RANK, MLA_ROPE_DIM, NSA_HEADS * NSA_HEAD_DIM) + (NSA_KV_HEADS * NSA_HEAD_DIM,) * 6 \
        + (3 * NSA_HEADS,)
    offs = [0]
    for sz in sizes:
        offs.append(offs[-1] + sz)
    seg = [w[:, offs[j]:offs[j + 1]] for j in range(len(sizes))]
    z = lambda n: jnp.zeros((d, n), w.dtype)
    kpe = jnp.concatenate([z(MLA_NOPE_DIM), seg[2], z(LANES - MLA_NOPE_DIM - MLA_ROPE_DIM)], axis=1)
    gates = jnp.concatenate([seg[10], z(LANES - 3 * NSA_HEADS)], axis=1)
    out = jnp.concatenate([seg[0], seg[1], kpe, seg[3]] + seg[4:10] + [gates], axis=1)
    assert out.shape[1] == C_TOTAL
    return out


def _layout_mla_up(w_q_up, w_kv_up):
    rq = w_q_up.shape[0]
    qd = MLA_NOPE_DIM + MLA_ROPE_DIM
    wq = w_q_up.reshape(rq, MLA_HEADS, qd)
    wq = jnp.pad(wq, ((0, 0), (0, 0), (0, LANES - qd))).reshape(rq, MLA_HEADS * LANES)
    rk = w_kv_up.shape[0]
    wkv = w_kv_up.reshape(rk, MLA_HEADS, MLA_NOPE_DIM + MLA_V_DIM)
    wk = jnp.pad(wkv[:, :, :MLA_NOPE_DIM], ((0, 0), (0, 0), (0, LANES - MLA_NOPE_DIM))).reshape(rk, MLA_HEADS * LANES)
    wv = wkv[:, :, MLA_NOPE_DIM:].reshape(rk, MLA_HEADS * MLA_V_DIM)
    return wq, wk, wv


def _layout_compress(cmp_pos, cmp_w1, cmp_w2):
    g = NSA_KV_HEADS
    eye = jnp.eye(g, dtype=F32)
    half = CMP_BLOCK // 2
    assert CMP_BLOCK == 2 * CMP_STRIDE

    def w1_part(w):
        return jnp.einsum('ldh,gk->lgdkh', w, eye).reshape(half * g * NSA_HEAD_DIM, g * CMP_HIDDEN)

    def pos_part(p):
        return jnp.broadcast_to(p[:, None, :], (half, g, NSA_HEAD_DIM)).reshape(1, -1)

    w1a = jnp.stack([w1_part(cmp_w1[i, :half]) for i in range(2)]).astype(BF16)
    w1b = jnp.stack([w1_part(cmp_w1[i, half:]) for i in range(2)]).astype(BF16)
    w2 = jnp.stack([jnp.einsum('hd,gk->ghkd', cmp_w2[i], eye).reshape(g * CMP_HIDDEN, g * NSA_HEAD_DIM)
                    for i in range(2)]).astype(BF16)
    pos = jnp.stack([jnp.concatenate([pos_part(cmp_pos[i, :half]), pos_part(cmp_pos[i, half:])], axis=0)
                     for i in range(2)])
    return pos, w1a, w1b, w2


def _overlap_matrix(rows, n_cmp):
    n = jnp.arange(rows)[:, None]
    j = jnp.arange(LANES)[None, :] % HALF
    start = n * CMP_STRIDE
    ov = (start < (j + 1) * SEL_BLOCK) & (start + CMP_BLOCK > j * SEL_BLOCK) & (n < n_cmp)
    return ov.astype(BF16)


def _gate_expand_matrix():
    width = NSA_HEADS * NSA_HEAD_DIM
    rows = jnp.arange(LANES)[:, None]
    cols = jnp.arange(3 * width)[None, :]
    branch, head = cols // width, (cols % width) // NSA_HEAD_DIM
    return (rows == head * 3 + branch).astype(BF16)


def kernel(x, positions, attn_norm, w_in, mla_q_norm, mla_w_q_up, mla_kv_norm, mla_w_kv_up, nsa_cmp_pos,
           nsa_cmp_w1, nsa_cmp_w2, w_out, ffn_norm, router_w, router_b, moe_w1, moe_b1, moe_w2, moe_b2, final_norm):
    b, s, d = x.shape
    n = b * s
    assert attn_norm.shape[0] == 1, "single-layer configuration"
    assert d == D_MODEL and s % SEL_BLOCK == 0 and s // SEL_BLOCK <= HALF and s % ATT_T == 0
    assert s % PROJ_TM == 0 and ATT_T % PROJ_TM == 0 and n % POST_TM == 0 and n % COMB_TM == 0

    mla_c, mla_s = _rope_tables(positions.reshape(n), MLA_ROPE_DIM, MLA_NOPE_DIM, LANES)
    nsa_c, nsa_s = _rope_tables(positions.reshape(n), NSA_ROPE_DIM, 0, NSA_HEAD_DIM)
    rows = s // CMP_STRIDE
    n_cmp = (s - CMP_BLOCK) // CMP_STRIDE + 1
    cmp_end = jnp.minimum(jnp.arange(rows) * CMP_STRIDE + CMP_BLOCK - 1, s - 1)
    cmp_c, cmp_s = _rope_tables(positions[:, cmp_end], NSA_ROPE_DIM, 0, NSA_HEAD_DIM)
    ov = _overlap_matrix(rows, n_cmp)
    eg = _gate_expand_matrix()

    x2 = x.reshape(n, d)
    win = _layout_w_in(w_in[0]).astype(BF16)
    wq, wk, wv = (w.astype(BF16) for w in _layout_mla_up(mla_w_q_up[0], mla_w_kv_up[0]))
    (qm, km, vmt, qs, kc_in, vc_in, ksx, vst, kwx, vwt, gl) = _proj(
        x2, (mla_c, mla_s, nsa_c, nsa_s), attn_norm[0][None], win, mla_q_norm[0][None], wq,
        mla_kv_norm[0][None], wk, wv, b, s)

    pos, w1a, w1b, w2c = _layout_compress(nsa_cmp_pos[0], nsa_cmp_w1[0], nsa_cmp_w2[0])
    flat = CMP_STRIDE * NSA_KV_HEADS * NSA_HEAD_DIM
    kcx, vcx = _compress(kc_in.reshape(b, rows, flat), vc_in.reshape(b, rows, flat), pos, w1a, w1b, w2c,
                         cmp_c, cmp_s)

    o_mla = _mla_attention(qm.reshape(b, s, -1), km.reshape(b, s, -1), vmt)
    o_cmp, qa = _cmp_attention(qs.reshape(b, s, -1), kcx, vcx, ov)
    o_sel, o_win = _selwin_attention(qa, ksx.reshape(b, s, -1), vst, kwx.reshape(b, s, -1), vwt)

    rw = jnp.pad(router_w[0], ((0, 0), (0, LANES - N_EXPERTS)))
    rw_hi = rw.astype(BF16)
    rw_lo = (rw - rw_hi.astype(F32)).astype(BF16)
    rb = jnp.pad(router_b[0], (0, LANES - N_EXPERTS))[None]
    hres, xn, route, counts = _post(
        x2, o_mla.reshape(n, -1), o_cmp.reshape(n, -1), o_sel.reshape(n, -1), o_win.reshape(n, -1), gl, eg,
        w_out[0].astype(BF16), ffn_norm[0][None], rw_hi, rw_lo, rb)

    top_idx = route[:, 0:TOP_K].astype(I32)
    rank = route[:, 2 * TOP_K:3 * TOP_K].astype(I32)
    cnt = counts[0, :N_EXPERTS].astype(I32)
    padded = ((cnt + FFN_BM - 1) // FFN_BM) * FFN_BM
    pend = jnp.cumsum(padded)
    pstart = pend - padded
    dest = pstart[top_idx] + rank
    nb = -(-(n * TOP_K) // FFN_BM) + N_EXPERTS + 1
    block_row = jnp.arange(nb, dtype=I32) * FFN_BM
    block_expert = jnp.minimum(jnp.sum((block_row[:, None] >= pend[None, :]).astype(I32), axis=1), N_EXPERTS - 1)
    block_valid = (block_row < pend[-1]).astype(I32)
    tok_ids = jnp.broadcast_to(jnp.arange(n, dtype=I32)[:, None], (n, TOP_K))
    row_tok = jnp.zeros((nb * FFN_BM,), I32).at[dest.reshape(-1)].set(tok_ids.reshape(-1), unique_indices=True)

    ys = _ffn(block_expert, block_valid, row_tok, xn, moe_w1[0].astype(BF16), moe_b1[0][:, None, :],
              moe_w2[0].astype(BF16), moe_b2[0][:, None, :])
    dest_t = dest.reshape(n // COMB_TM, COMB_TM, TOP_K).transpose(0, 2, 1).reshape(n // COMB_TM, 1, TOP_K * COMB_TM)
    out = _combine(dest_t, hres, route, final_norm[None], ys)
    return out.reshape(b, s, d)
```

```python
import functools
import math

import jax
import jax.numpy as jnp
from jax import lax
from jax.experimental import pallas as pl
from jax.experimental.pallas import tpu as pltpu

F32 = jnp.float32
BF16 = jnp.bfloat16
I32 = jnp.int32

D_MODEL = 1024
ROPE_THETA = 500000.0
NORM_EPS = 1e-5
NEG_INF = -1e30
POS_INF = 1e30
LOG2E = math.log2(math.e)

MLA_HEADS = 8
MLA_NOPE_DIM = 64
MLA_ROPE_DIM = 32
MLA_V_DIM = 64
MLA_Q_RANK = 256
MLA_KV_RANK = 128

NSA_HEADS = 8
NSA_KV_HEADS = 2
NSA_GROUP = NSA_HEADS // NSA_KV_HEADS
NSA_HEAD_DIM = 64
NSA_ROPE_DIM = NSA_HEAD_DIM // 4
CMP_BLOCK = 32
CMP_STRIDE = 16
CMP_HIDDEN = 2 * NSA_HEAD_DIM
SEL_BLOCK = 64
N_SEL = 16
N_LOCAL_SEL = 2
WINDOW = 512

N_EXPERTS = 32
TOP_K = 4
D_FF = D_MODEL
SWIGLU_LIMIT = 7.0
SWIGLU_ALPHA = 1.702

LANES = 128
HALF = LANES // 2
VMEM_LIMIT = 48 * 1024 * 1024

PROJ_TM = 256
ATT_T = 512
ATT_QS = 256
ATT_KC_FULL = 512
ATT_KC_DIAG = 256
ATT_AHEAD = 4
V_PAD = 16
CMP_TQ = 256
POST_TM = 256
FFN_BM = 512
COMB_TM = 128
FFN_CHUNKS = 4
DMA_UNROLL = 8
ROW_CHUNKS = D_MODEL // LANES

C_QLAT = 0
C_KVLAT = C_QLAT + MLA_Q_RANK
C_KPE = C_KVLAT + MLA_KV_RANK
C_QNSA = C_KPE + LANES
C_KC = C_QNSA + NSA_HEADS * NSA_HEAD_DIM
C_VC = C_KC + LANES
C_KS = C_VC + LANES
C_VS = C_KS + LANES
C_KW = C_VS + LANES
C_VW = C_KW + LANES
C_GATE = C_VW + LANES
C_TOTAL = C_GATE + LANES


def _cparams(sem):
    return pltpu.CompilerParams(dimension_semantics=sem, vmem_limit_bytes=VMEM_LIMIT)


def _rms(x, g):
    return x * lax.rsqrt(jnp.mean(x * x, axis=-1, keepdims=True) + NORM_EPS) * g


def _sigmoid(x):
    return 1.0 / (1.0 + jnp.exp(-x))


def _rope(x, c, s, half, first):
    n = x.shape[-1]
    partner = jnp.where(first, pltpu.roll(x, n - half, 1), pltpu.roll(x, half, 1))
    return x * c + partner * s


def _expand_pair(x, lo, fill):
    y = pltpu.roll(x, HALF, 1)
    return (jnp.where(lo, x, fill), jnp.where(lo, fill, y), jnp.where(lo, y, fill), jnp.where(lo, fill, x))


def _proj_kernel(x_ref, mc_ref, ms_ref, nc_ref, ns_ref, an_ref, win_ref, qn_ref, wq_ref, kvn_ref, wk_ref,
                 wv_ref, qm_ref, km_ref, vm_ref, qs_ref, kc_ref, vc_ref, ks_ref, vs_ref, kw_ref, vw_ref,
                 g_ref, *, tm, seq):
    x = x_ref[...]
    u = _rms(x, an_ref[...])
    y = jnp.dot(u.astype(BF16), win_ref[...], preferred_element_type=F32)
    lane = lax.broadcasted_iota(I32, (1, LANES), 1)
    lo = lane < HALF
    mla_first = lane < MLA_NOPE_DIM + MLA_ROPE_DIM // 2
    nsa_first = (lane % NSA_HEAD_DIM) < NSA_ROPE_DIM // 2
    mc, ms, nc, ns = mc_ref[...], ms_ref[...], nc_ref[...], ns_ref[...]

    qn = _rms(y[:, C_QLAT:C_QLAT + MLA_Q_RANK], qn_ref[...])
    q = jnp.dot(qn.astype(BF16), wq_ref[...], preferred_element_type=F32)
    q = q * ((MLA_NOPE_DIM + MLA_ROPE_DIM) ** -0.5 * LOG2E)
    for h in range(MLA_HEADS):
        sl = slice(h * LANES, (h + 1) * LANES)
        qm_ref[:, sl] = _rope(q[:, sl], mc, ms, MLA_ROPE_DIM // 2, mla_first).astype(BF16)
    kvn = _rms(y[:, C_KVLAT:C_KVLAT + MLA_KV_RANK], kvn_ref[...]).astype(BF16)
    kpe = _rope(y[:, C_KPE:C_KPE + LANES], mc, ms, MLA_ROPE_DIM // 2, mla_first)
    kn = jnp.dot(kvn, wk_ref[...], preferred_element_type=F32)
    for h in range(MLA_HEADS):
        sl = slice(h * LANES, (h + 1) * LANES)
        km_ref[:, sl] = (kn[:, sl] + kpe).astype(BF16)
    ones_pad = (lax.broadcasted_iota(I32, (V_PAD, tm), 0) == 0).astype(BF16)
    vmt = jnp.dot(kvn, wv_ref[...], preferred_element_type=F32).T.astype(BF16)
    for h in range(MLA_HEADS):
        base = h * (MLA_V_DIM + V_PAD)
        vm_ref[0, 0, base:base + MLA_V_DIM, :] = vmt[h * MLA_V_DIM:(h + 1) * MLA_V_DIM]
        vm_ref[0, 0, base + MLA_V_DIM:base + MLA_V_DIM + V_PAD, :] = ones_pad

    for c in range(NSA_HEADS * NSA_HEAD_DIM // LANES):
        ch = y[:, C_QNSA + c * LANES:C_QNSA + (c + 1) * LANES]
        qs_ref[:, c * LANES:(c + 1) * LANES] = (
            _rope(ch, nc, ns, NSA_ROPE_DIM // 2, nsa_first) * (NSA_HEAD_DIM ** -0.5 * LOG2E)).astype(BF16)
    kc_ref[...] = y[:, C_KC:C_KC + LANES]
    vc_ref[...] = y[:, C_VC:C_VC + LANES]
    row = lax.broadcasted_iota(I32, (tm, LANES), 0)
    tok = (pl.program_id(0) * tm + row) % seq
    onehot = ((lane % HALF) == tok // SEL_BLOCK).astype(F32)
    ks = _rope(y[:, C_KS:C_KS + LANES], nc, ns, NSA_ROPE_DIM // 2, nsa_first)
    kw = _rope(y[:, C_KW:C_KW + LANES], nc, ns, NSA_ROPE_DIM // 2, nsa_first)
    for ref, val, fill in ((ks_ref, ks, onehot), (kw_ref, kw, 0.0)):
        for c, chunk in enumerate(_expand_pair(val, lo, fill)):
            ref[:, c * LANES:(c + 1) * LANES] = chunk.astype(BF16)
    for ref, col in ((vs_ref, C_VS), (vw_ref, C_VW)):
        vt = y[:, col:col + LANES].T.astype(BF16)
        for g in range(NSA_KV_HEADS):
            base = g * (NSA_HEAD_DIM + V_PAD)
            ref[0, 0, base:base + NSA_HEAD_DIM, :] = vt[g * NSA_HEAD_DIM:(g + 1) * NSA_HEAD_DIM]
            ref[0, 0, base + NSA_HEAD_DIM:base + NSA_HEAD_DIM + V_PAD, :] = ones_pad
    g_ref[...] = y[:, C_GATE:C_GATE + LANES]


def _proj(x2, tabs, an, win, qn, wq, kvn, wk, wv, batch, seq):
    n = x2.shape[0]
    tm = PROJ_TM
    per_b, per_slab = seq // tm, ATT_T // tm
    tok = lambda w: pl.BlockSpec((tm, w), lambda i: (i, 0))
    tr = lambda r: pl.BlockSpec((1, 1, r, tm),
                                lambda i: (i // per_b, (i % per_b) // per_slab, 0, (i % per_b) % per_slab))
    full = lambda a: pl.BlockSpec(a.shape, lambda i: (0,) * a.ndim)
    tok_out = lambda w, d: (tok(w), jax.ShapeDtypeStruct((n, w), d))
    tr_out = lambda r: (tr(r), jax.ShapeDtypeStruct((batch, seq // ATT_T, r, ATT_T), BF16))
    outs = [tok_out(8 * LANES, BF16), tok_out(8 * LANES, BF16), tr_out(MLA_HEADS * (MLA_V_DIM + V_PAD)),
            tok_out(4 * LANES, BF16), tok_out(LANES, F32), tok_out(LANES, F32),
            tok_out(4 * LANES, BF16), tr_out(NSA_KV_HEADS * (NSA_HEAD_DIM + V_PAD)), tok_out(4 * LANES, BF16),
            tr_out(NSA_KV_HEADS * (NSA_HEAD_DIM + V_PAD)), tok_out(LANES, F32)]
    return pl.pallas_call(
        functools.partial(_proj_kernel, tm=tm, seq=seq),
        grid=(n // tm,),
        in_specs=[tok(D_MODEL)] + [tok(LANES)] * 4 + [full(a) for a in (an, win, qn, wq, kvn, wk, wv)],
        out_specs=[o[0] for o in outs],
        out_shape=[o[1] for o in outs],
        compiler_params=_cparams(("parallel",)),
        name="proj",
    )(x2, *tabs, an, win, qn, wq, kvn, wk, wv)


def _compress_kernel(kin_ref, vin_ref, pos_ref, w1a_ref, w1b_ref, w2_ref, cc_ref, cs_ref, kc_ref, vc_ref, *, rows):
    lane = lax.broadcasted_iota(I32, (1, LANES), 1)
    lo = lane < HALF
    nsa_first = (lane % NSA_HEAD_DIM) < NSA_ROPE_DIM // 2

    def comp(x, i):
        p = jnp.dot((x + pos_ref[i, 0:1, :]).astype(BF16), w1a_ref[i], preferred_element_type=F32)
        q = jnp.dot((x + pos_ref[i, 1:2, :]).astype(BF16), w1b_ref[i], preferred_element_type=F32)
        hid = p + pltpu.roll(q, rows - 1, 0)
        hid = hid * _sigmoid(hid)
        return jnp.dot(hid.astype(BF16), w2_ref[i], preferred_element_type=F32)

    kc = _rope(comp(kin_ref[0], 0), cc_ref[0], cs_ref[0], NSA_ROPE_DIM // 2, nsa_first)
    vc = comp(vin_ref[0], 1)
    for ref, val in ((kc_ref, kc), (vc_ref, vc)):
        for c, chunk in enumerate(_expand_pair(val, lo, 0.0)):
            ref[0, :, c * LANES:(c + 1) * LANES] = chunk.astype(BF16)


def _compress(kin, vin, pos, w1a, w1b, w2, cc, cs):
    b, rows, width = kin.shape
    per_b = lambda w: pl.BlockSpec((1, rows, w), lambda i: (i, 0, 0))
    full = lambda a: pl.BlockSpec(a.shape, lambda i: (0,) * a.ndim)
    return pl.pallas_call(
        functools.partial(_compress_kernel, rows=rows),
        grid=(b,),
        in_specs=[per_b(width), per_b(width), full(pos), full(w1a), full(w1b), full(w2), per_b(LANES), per_b(LANES)],
        out_specs=[per_b(4 * LANES), per_b(4 * LANES)],
        out_shape=[jax.ShapeDtypeStruct((b, rows, 4 * LANES), BF16)] * 2,
        compiler_params=_cparams(("parallel",)),
        name="compress",
    )(kin, vin, pos, w1a, w1b, w2, cc, cs)


def _subtile_plan(t, kc, delta, window):
    plan = []
    for j in range(t // ATT_QS):
        for c in range(t // kc):
            if delta is None:
                plan.append((j, c, False))
                continue
            dmin = delta + j * ATT_QS - (c * kc + kc - 1)
            dmax = delta + j * ATT_QS + ATT_QS - 1 - c * kc
            if dmax < 0 or (window is not None and dmin >= window):
                continue
            full = dmin >= 0 and (window is None or dmax < window)
            plan.append((j, c, not full))
    return plan


def _flash_tiles(q_of, heads, t, tiles):
    jsl = lambda j: slice(j * ATT_QS, (j + 1) * ATT_QS)
    items = []
    for tile in tiles:
        plan = _subtile_plan(t, tile["kc"], tile["delta"], tile["window"])
        for h in heads:
            for j in sorted({jj for jj, _, _ in plan}):
                items += [(tile, h, j, c, masked) for jj, c, masked in plan if jj == j]
    keys = []
    for tile, h, j, _, _ in items:
        key = (id(tile["m_ref"]), h, j)
        if key not in [k for k, _ in keys]:
            keys.append((key, (tile["m_ref"], tile["acc_ref"], h, j)))
    state = {key: (m_ref[h:h + 1, jsl(j)], acc_ref[h, :, jsl(j)]) for key, (m_ref, acc_ref, h, j) in keys}

    def scores(item):
        tile, h, j, c, masked = item
        kc = tile["kc"]
        st = lax.dot_general(tile["k_of"](h, c, kc), q_of(h, j), (((1,), (1,)), ((), ())),
                             preferred_element_type=F32)
        if masked:
            kk = lax.broadcasted_iota(I32, (kc, ATT_QS), 0)
            qq = lax.broadcasted_iota(I32, (kc, ATT_QS), 1)
            d = (tile["delta"] + j * ATT_QS - c * kc) + qq - kk
            ok = d >= 0
            if tile["window"] is not None:
                ok = ok & (d < tile["window"])
            st = jnp.where(ok, st, NEG_INF)
        return st

    def fold(pending):
        key, a_prev, pv_prev = pending
        mm, aa = state[key]
        state[key] = (mm, aa * a_prev + pv_prev)

    ahead = [scores(it) for it in items[:ATT_AHEAD]]
    pending = None
    for n, (tile, h, j, c, _) in enumerate(items):
        st = ahead.pop(0)
        if n + ATT_AHEAD < len(items):
            ahead.append(scores(items[n + ATT_AHEAD]))
        key = (id(tile["m_ref"]), h, j)
        m = state[key][0]
        m_new = jnp.maximum(m, jnp.max(st, axis=0, keepdims=True))
        alpha = jnp.exp2(m - m_new)
        p = jnp.exp2((st - m_new).astype(BF16))
        pv = jnp.dot(tile["vt_of"](h, c, tile["kc"]), p, preferred_element_type=F32)
        if pending is not None:
            fold(pending)
        state[key] = (m_new, state[key][1])
        pending = (key, alpha, pv)
    fold(pending)
    for key, (m_ref, acc_ref, h, j) in keys:
        m, acc = state[key]
        m_ref[h:h + 1, jsl(j)] = m
        acc_ref[h, :, jsl(j)] = acc


def _flash_init(m_ref, acc_ref):
    m_ref[...] = jnp.full(m_ref.shape, NEG_INF, F32)
    acc_ref[...] = jnp.zeros(acc_ref.shape, F32)


def _flash_out_pair(acc_ref, pair, dv):
    parts = [acc_ref[h, 0:dv, :] / acc_ref[h, dv:dv + 1, :] for h in (2 * pair, 2 * pair + 1)]
    return jnp.concatenate(parts, axis=0).T


def _kv_readers(k_ref, vt_ref, kt, t, k_chunk_of_head, v_rows_of_head):
    def k_of(h, c, kc):
        start = pl.multiple_of(kt * t + c * kc, kc)
        return k_ref[0, pl.ds(start, kc), k_chunk_of_head(h)]

    def vt_of(h, c, kc):
        return vt_ref[0, kt, v_rows_of_head(h), c * kc:(c + 1) * kc]

    return k_of, vt_of


def _mla_kernel(q_ref, k_ref, vt_ref, o_ref, m_ref, acc_ref, *, t):
    qi = pl.program_id(2)
    heads = (0, 1)
    chunk = lambda h: slice(h * LANES, (h + 1) * LANES)
    vrows = lambda h: slice(h * (MLA_V_DIM + V_PAD), (h + 1) * (MLA_V_DIM + V_PAD))
    q_of = lambda h, j: q_ref[0, j * ATT_QS:(j + 1) * ATT_QS, chunk(h)]
    _flash_init(m_ref, acc_ref)

    def tile(kt, kc, delta):
        k_of, vt_of = _kv_readers(k_ref, vt_ref, kt, t, chunk, vrows)
        return dict(k_of=k_of, vt_of=vt_of, kc=kc, delta=delta, window=None, m_ref=m_ref, acc_ref=acc_ref)

    def body(kp, carry):
        _flash_tiles(q_of, heads, t, [tile(2 * kp, ATT_KC_FULL, None), tile(2 * kp + 1, ATT_KC_FULL, None)])
        return carry

    lax.fori_loop(0, qi // 2, body, 0)

    @pl.when(qi % 2 == 1)
    def _():
        _flash_tiles(q_of, heads, t, [tile(qi - 1, ATT_KC_FULL, None), tile(qi, ATT_KC_DIAG, 0)])

    @pl.when(qi % 2 == 0)
    def _():
        _flash_tiles(q_of, heads, t, [tile(qi, ATT_KC_DIAG, 0)])

    o_ref[0] = _flash_out_pair(acc_ref, 0, MLA_V_DIM)


def _mla_attention(q, k, vt):
    b, s, _ = q.shape
    t = ATT_T
    pairs = MLA_HEADS // 2
    return pl.pallas_call(
        functools.partial(_mla_kernel, t=t),
        grid=(b, pairs, s // t),
        in_specs=[pl.BlockSpec((1, t, 2 * LANES), lambda bi, p, qi: (bi, qi, p)),
                  pl.BlockSpec((1, s, 2 * LANES), lambda bi, p, qi: (bi, 0, p)),
                  pl.BlockSpec((1, s // t, 2 * (MLA_V_DIM + V_PAD), t), lambda bi, p, qi: (bi, 0, p, 0))],
        out_specs=pl.BlockSpec((1, t, LANES), lambda bi, p, qi: (bi, qi, p)),
        out_shape=jax.ShapeDtypeStruct((b, s, pairs * LANES), F32),
        scratch_shapes=[pltpu.VMEM((2, t), F32), pltpu.VMEM((2, MLA_V_DIM + V_PAD, t), F32)],
        compiler_params=_cparams(("parallel", "parallel", "arbitrary")),
        name="mla_attention",
    )(q, k, vt)


def _cmp_kernel(q_ref, kc_ref, vc_ref, ov_ref, o_ref, qa_ref, *, tq, rows):
    qi = pl.program_id(2)
    lane = lax.broadcasted_iota(I32, (1, LANES), 1)
    lo = lane < HALF
    t_pos = qi * tq + lax.broadcasted_iota(I32, (tq, 1), 0)
    cmp_end = lax.broadcasted_iota(I32, (1, rows), 1) * CMP_STRIDE + (CMP_BLOCK - 1)
    mask = cmp_end <= t_pos
    live = (t_pos >= CMP_BLOCK - 1).astype(F32)

    psum = jnp.zeros((tq, rows), F32)
    for pair in range(NSA_GROUP // 2):
        qp = q_ref[0, :, pair * LANES:(pair + 1) * LANES]
        acc = jnp.zeros((tq, LANES), F32)
        for e in range(2):
            sl = slice(e * LANES, (e + 1) * LANES)
            s = lax.dot_general(qp, kc_ref[0, :, sl], (((1,), (1,)), ((), ())), preferred_element_type=F32)
            s = jnp.where(mask, s, NEG_INF)
            p = jnp.exp2(s - jnp.max(s, axis=-1, keepdims=True))
            p = p / jnp.sum(p, axis=-1, keepdims=True) * live
            psum = psum + p
            acc = acc + jnp.dot(p.astype(BF16), vc_ref[0, :, sl], preferred_element_type=F32)
        o_ref[0, :, pair * LANES:(pair + 1) * LANES] = acc

    p_hi = psum.astype(BF16)
    p_lo = (psum - p_hi.astype(F32)).astype(BF16)
    imp = (jnp.dot(p_hi, ov_ref[...], preferred_element_type=F32)
           + jnp.dot(p_lo, ov_ref[...], preferred_element_type=F32))
    blk = lane % HALF
    cur = t_pos // SEL_BLOCK
    forced = (blk == 0) | ((blk <= cur) & (blk > cur - N_LOCAL_SEL))
    val = jnp.where(forced, POS_INF, jnp.where(blk <= cur, imp, NEG_INF))

    vt = val.T[:HALF]
    sub = 8
    groups = [vt[a * sub:(a + 1) * sub] for a in range(HALF // sub)]
    cnts = [jnp.zeros((sub, tq), I32) for _ in groups]
    jsub = lax.broadcasted_iota(I32, (sub, tq), 0)
    for i in range(HALF):
        r = vt[i:i + 1, :]
        for a, grp in enumerate(groups):
            if a > i // sub:
                ahead = r >= grp
            elif a < i // sub:
                ahead = r > grp
            else:
                ahead = (r > grp) | ((r == grp) & (i % sub < jsub))
            cnts[a] = cnts[a] + ahead.astype(I32)
    cnt = jnp.concatenate(cnts, axis=0)
    bias_t = jnp.where(cnt < N_SEL, 0.0, NEG_INF).astype(F32)
    bias = jnp.concatenate([bias_t, bias_t], axis=0).T.astype(BF16)
    for h in range(NSA_GROUP):
        qp = q_ref[0, :, (h // 2) * LANES:(h // 2 + 1) * LANES]
        chunk = jnp.where(lo, qp, bias) if h % 2 == 0 else jnp.where(lo, bias, qp)
        qa_ref[0, :, h * LANES:(h + 1) * LANES] = chunk


def _cmp_attention(q, kcx, vcx, ov):
    b, s, _ = q.shape
    rows = kcx.shape[1]
    tq = min(CMP_TQ, s)
    g = NSA_KV_HEADS
    return pl.pallas_call(
        functools.partial(_cmp_kernel, tq=tq, rows=rows),
        grid=(b, g, s // tq),
        in_specs=[pl.BlockSpec((1, tq, 2 * LANES), lambda bi, gi, qi: (bi, qi, gi)),
                  pl.BlockSpec((1, rows, 2 * LANES), lambda bi, gi, qi: (bi, 0, gi)),
                  pl.BlockSpec((1, rows, 2 * LANES), lambda bi, gi, qi: (bi, 0, gi)),
                  pl.BlockSpec(ov.shape, lambda bi, gi, qi: (0, 0))],
        out_specs=[pl.BlockSpec((1, tq, 2 * LANES), lambda bi, gi, qi: (bi, qi, gi)),
                   pl.BlockSpec((1, tq, 4 * LANES), lambda bi, gi, qi: (bi, qi, gi))],
        out_shape=[jax.ShapeDtypeStruct((b, s, g * 2 * LANES), F32),
                   jax.ShapeDtypeStruct((b, s, g * 4 * LANES), BF16)],
        compiler_params=_cparams(("parallel", "parallel", "parallel")),
        name="cmp_attention",
    )(q, kcx, vcx, ov)


def _selwin_kernel(qa_ref, ks_ref, vs_ref, kw_ref, vw_ref, os_ref, ow_ref, ms_ref, as_ref, mw_ref, aw_ref, *, t):
    qi = pl.program_id(2)
    heads = tuple(range(NSA_GROUP))
    chunk = lambda h: slice((h % 2) * LANES, (h % 2 + 1) * LANES)
    vrows = lambda h: slice(0, NSA_HEAD_DIM + V_PAD)
    q_of = lambda h, j: qa_ref[0, j * ATT_QS:(j + 1) * ATT_QS, h * LANES:(h + 1) * LANES]
    _flash_init(ms_ref, as_ref)
    _flash_init(mw_ref, aw_ref)

    def sel_tile(kt, kc, delta):
        k_of, vt_of = _kv_readers(ks_ref, vs_ref, kt, t, chunk, vrows)
        return dict(k_of=k_of, vt_of=vt_of, kc=kc, delta=delta, window=None, m_ref=ms_ref, acc_ref=as_ref)

    def win_tile(kt, delta):
        k_of, vt_of = _kv_readers(kw_ref, vw_ref, kt, t, chunk, vrows)
        return dict(k_of=k_of, vt_of=vt_of, kc=ATT_KC_DIAG, delta=delta, window=WINDOW, m_ref=mw_ref, acc_ref=aw_ref)

    def body(kt, carry):
        _flash_tiles(q_of, heads, t, [sel_tile(kt, ATT_KC_FULL, None)])
        return carry

    lax.fori_loop(0, qi, body, 0)

    @pl.when(qi > 0)
    def _():
        _flash_tiles(q_of, heads, t, [win_tile(qi - 1, t), sel_tile(qi, ATT_KC_DIAG, 0), win_tile(qi, 0)])

    @pl.when(qi == 0)
    def _():
        _flash_tiles(q_of, heads, t, [sel_tile(qi, ATT_KC_DIAG, 0), win_tile(qi, 0)])

    for pair in range(NSA_GROUP // 2):
        sl = slice(pair * LANES, (pair + 1) * LANES)
        os_ref[0, :, sl] = _flash_out_pair(as_ref, pair, NSA_HEAD_DIM)
        ow_ref[0, :, sl] = _flash_out_pair(aw_ref, pair, NSA_HEAD_DIM)


def _selwin_attention(qa, ksx, vst, kwx, vwt):
    b, s, _ = qa.shape
    t = ATT_T
    assert t >= WINDOW, "window branch reads only the previous and the diagonal key tile"
    g = NSA_KV_HEADS
    pairs = NSA_GROUP // 2
    kspec = pl.BlockSpec((1, s, 2 * LANES), lambda bi, gi, qi: (bi, 0, gi))
    vspec = pl.BlockSpec((1, s // t, NSA_HEAD_DIM + V_PAD, t), lambda bi, gi, qi: (bi, 0, gi, 0))
    ospec = pl.BlockSpec((1, t, pairs * LANES), lambda bi, gi, qi: (bi, qi, gi))
    stat = lambda: pltpu.VMEM((NSA_GROUP, t), F32)
    accs = lambda: pltpu.VMEM((NSA_GROUP, NSA_HEAD_DIM + V_PAD, t), F32)
    return pl.pallas_call(
        functools.partial(_selwin_kernel, t=t),
        grid=(b, g, s // t),
        in_specs=[pl.BlockSpec((1, t, NSA_GROUP * LANES), lambda bi, gi, qi: (bi, qi, gi)),
                  kspec, vspec, kspec, vspec],
        out_specs=[ospec, ospec],
        out_shape=[jax.ShapeDtypeStruct((b, s, g * pairs * LANES), F32)] * 2,
        scratch_shapes=[stat(), accs(), stat(), accs()],
        compiler_params=_cparams(("parallel", "parallel", "arbitrary")),
        name="selwin_attention",
    )(qa, ksx, vst, kwx, vwt)


def _post_kernel(x_ref, om_ref, oc_ref, os_ref, ow_ref, g_ref, eg_ref, wo_ref, fn_ref, rwh_ref, rwl_ref, rb_ref,
                 h_ref, xn_ref, route_ref, cnt_ref, *, tm):
    i = pl.program_id(0)
    width = NSA_HEADS * NSA_HEAD_DIM
    sg = _sigmoid(g_ref[...])
    sg_hi = sg.astype(BF16)
    sg_lo = (sg - sg_hi.astype(F32)).astype(BF16)
    ge = (jnp.dot(sg_hi, eg_ref[...], preferred_element_type=F32)
          + jnp.dot(sg_lo, eg_ref[...], preferred_element_type=F32))
    o_nsa = (ge[:, 0:width] * oc_ref[...] + ge[:, width:2 * width] * os_ref[...]
             + ge[:, 2 * width:3 * width] * ow_ref[...])
    mla_w = MLA_HEADS * MLA_V_DIM
    mixed = (jnp.dot(om_ref[...].astype(BF16), wo_ref[0:mla_w, :], preferred_element_type=F32)
             + jnp.dot(o_nsa.astype(BF16), wo_ref[mla_w:mla_w + width, :], preferred_element_type=F32))
    h = x_ref[...] + mixed
    h_ref[...] = h
    xn = _rms(h, fn_ref[...])
    _lanes_to_rows(xn_ref, xn)

    x_hi = xn.astype(BF16)
    x_lo = (xn - x_hi.astype(F32)).astype(BF16)
    logits = (jnp.dot(x_hi, rwh_ref[...], preferred_element_type=F32)
              + jnp.dot(x_hi, rwl_ref[...], preferred_element_type=F32)
              + jnp.dot(x_lo, rwh_ref[...], preferred_element_type=F32)) + rb_ref[...]
    lane = lax.broadcasted_iota(I32, (tm, LANES), 1)
    lg = jnp.where(lane < N_EXPERTS, logits, -jnp.inf)
    vals, hots = [], []
    for _ in range(TOP_K):
        m = jnp.max(lg, axis=-1, keepdims=True)
        idx = jnp.min(jnp.where(lg == m, lane, LANES), axis=-1, keepdims=True)
        hot = lane == idx
        lg = jnp.where(hot, -jnp.inf, lg)
        vals.append(m)
        hots.append(hot)
    es = [jnp.exp(v - vals[0]) for v in vals]
    den = es[0] + es[1] + es[2] + es[3]

    @pl.when(i == 0)
    def _():
        cnt_ref[...] = jnp.zeros(cnt_ref.shape, F32)

    hot_all = (hots[0] | hots[1] | hots[2] | hots[3]).astype(F32)
    r = lax.broadcasted_iota(I32, (tm, tm), 0)
    c = lax.broadcasted_iota(I32, (tm, tm), 1)
    tri = (c < r).astype(BF16)
    before = jnp.dot(tri, hot_all.astype(BF16), preferred_element_type=F32) + cnt_ref[...]
    route = jnp.zeros((tm, LANES), F32)
    for k in range(TOP_K):
        e_k = jnp.sum(jnp.where(hots[k], lane, 0), axis=-1, keepdims=True).astype(F32)
        rank_k = jnp.sum(jnp.where(hots[k], before, 0.0), axis=-1, keepdims=True)
        route = (route + jnp.where(lane == k, e_k, 0.0) + jnp.where(lane == TOP_K + k, es[k] / den, 0.0)
                 + jnp.where(lane == 2 * TOP_K + k, rank_k, 0.0))
    route_ref[...] = route
    cnt_ref[...] = cnt_ref[...] + jnp.sum(hot_all, axis=0, keepdims=True)


def _post(x2, om, oc, os_, ow, gl, eg, wo, fn, rwh, rwl, rb):
    n = x2.shape[0]
    tm = POST_TM
    tok = lambda w: pl.BlockSpec((tm, w), lambda i: (i, 0))
    full = lambda a: pl.BlockSpec(a.shape, lambda i: (0,) * a.ndim)
    return pl.pallas_call(
        functools.partial(_post_kernel, tm=tm),
        grid=(n // tm,),
        in_specs=[tok(D_MODEL), tok(om.shape[1]), tok(oc.shape[1]), tok(os_.shape[1]), tok(ow.shape[1]), tok(LANES)]
        + [full(a) for a in (eg, wo, fn, rwh, rwl, rb)],
        out_specs=[tok(D_MODEL), pl.BlockSpec((tm * ROW_CHUNKS, LANES), lambda i: (i, 0)), tok(LANES),
                   pl.BlockSpec((1, LANES), lambda i: (0, 0))],
        out_shape=[jax.ShapeDtypeStruct((n, D_MODEL), F32), jax.ShapeDtypeStruct((n * ROW_CHUNKS, LANES), F32),
                   jax.ShapeDtypeStruct((n, LANES), F32), jax.ShapeDtypeStruct((1, LANES), F32)],
        compiler_params=_cparams(("arbitrary",)),
        name="post",
    )(x2, om, oc, os_, ow, gl, eg, wo, fn, rwh, rwl, rb)


def _gather_rows(idx_ref, src_hbm, dst, sem, first, count, unroll):
    def body(r, carry):
        src = pl.multiple_of(idx_ref[0, 0, first + r] * ROW_CHUNKS, ROW_CHUNKS)
        dst_row = pl.multiple_of((first + r) * ROW_CHUNKS, ROW_CHUNKS)
        pltpu.make_async_copy(src_hbm.at[pl.ds(src, ROW_CHUNKS), :], dst.at[pl.ds(dst_row, ROW_CHUNKS), :], sem).start()
        return carry
    lax.fori_loop(0, count, body, 0, unroll=unroll)


def _wait_rows(src_hbm, dst, sem, count):
    pltpu.make_async_copy(src_hbm.at[pl.ds(0, count * ROW_CHUNKS), :], dst, sem).wait()


def _rows_to_lanes(ref, lead, first, rows, dtype):
    return jnp.concatenate(
        [ref[lead + (pl.ds(first * ROW_CHUNKS + c, rows, stride=ROW_CHUNKS), slice(None))].astype(dtype)
         for c in range(ROW_CHUNKS)], axis=1)


def _lanes_to_rows(ref, val):
    for c in range(ROW_CHUNKS):
        ref[pl.ds(c, val.shape[0], stride=ROW_CHUNKS), :] = val[:, c * LANES:(c + 1) * LANES]


def _ffn_kernel(be_ref, bv_ref, idx0_ref, idxn_ref, xn_hbm, w1_ref, b1_ref, w2_ref, b2_ref, y_ref, xbuf, xb_ref, yacc_ref,
                sem, *, bm):
    del be_ref
    i = pl.program_id(0)
    slot = i % 2

    @pl.when(i == 0)
    def _():
        _gather_rows(idx0_ref, xn_hbm, xbuf.at[0], sem.at[0], 0, bm, DMA_UNROLL)

    @pl.when(bv_ref[i] > 0)
    def _():
        _wait_rows(xn_hbm, xbuf.at[slot], sem.at[slot], bm)
        xb_ref[...] = _rows_to_lanes(xbuf, (slot,), 0, bm, BF16)
        yacc_ref[...] = jnp.broadcast_to(b2_ref[0], yacc_ref.shape)
        per = bm // FFN_CHUNKS
        fc = D_FF // FFN_CHUNKS

        def chunk(c, carry):
            _gather_rows(idxn_ref, xn_hbm, xbuf.at[1 - slot], sem.at[1 - slot], c * per, per, True)
            hc = jnp.dot(xb_ref[...], w1_ref[0, c], preferred_element_type=F32) + b1_ref[0, c]
            a = jnp.minimum(hc[:, :fc], SWIGLU_LIMIT)
            up = jnp.clip(hc[:, fc:], -SWIGLU_LIMIT, SWIGLU_LIMIT)
            glu = a * _sigmoid(SWIGLU_ALPHA * a)
            yacc_ref[...] += jnp.dot(((up + 1.0) * glu).astype(BF16), w2_ref[0, c], preferred_element_type=F32)
            return carry

        lax.fori_loop(0, FFN_CHUNKS, chunk, 0)
        _lanes_to_rows(y_ref, yacc_ref[...])

    @pl.when(bv_ref[i] == 0)
    def _():
        @pl.when(bv_ref[jnp.maximum(i - 1, 0)] > 0)
        def _():
            _wait_rows(xn_hbm, xbuf.at[slot], sem.at[slot], bm)

        y_ref[...] = jnp.zeros(y_ref.shape, F32)


def _ffn(block_expert, block_valid, row_tok, xn, w1, b1, w2, b2):
    nb = block_expert.shape[0]
    bm = FFN_BM
    idx3 = row_tok.reshape(nb, 1, bm)
    smem_blk = lambda fn: pl.BlockSpec((1, 1, bm), fn, memory_space=pltpu.SMEM)
    grid_spec = pltpu.PrefetchScalarGridSpec(
        num_scalar_prefetch=2,
        grid=(nb,),
        in_specs=[smem_blk(lambda i, be, bv: (0, 0, 0)),
                  smem_blk(lambda i, be, bv: (jnp.minimum(i + 1, nb - 1), 0, 0)),
                  pl.BlockSpec(memory_space=pl.ANY),
                  pl.BlockSpec((1,) + w1.shape[1:], lambda i, be, bv: (be[i], 0, 0, 0)),
                  pl.BlockSpec((1,) + b1.shape[1:], lambda i, be, bv: (be[i], 0, 0, 0)),
                  pl.BlockSpec((1,) + w2.shape[1:], lambda i, be, bv: (be[i], 0, 0, 0)),
                  pl.BlockSpec((1, 1, D_MODEL), lambda i, be, bv: (be[i], 0, 0))],
        out_specs=pl.BlockSpec((bm * ROW_CHUNKS, LANES), lambda i, be, bv: (i, 0)),
        scratch_shapes=[pltpu.VMEM((2, bm * ROW_CHUNKS, LANES), F32), pltpu.VMEM((bm, D_MODEL), BF16),
                        pltpu.VMEM((bm, D_MODEL), F32), pltpu.SemaphoreType.DMA((2,))],
    )
    return pl.pallas_call(
        functools.partial(_ffn_kernel, bm=bm),
        grid_spec=grid_spec,
        out_shape=jax.ShapeDtypeStruct((nb * bm * ROW_CHUNKS, LANES), F32),
        compiler_params=_cparams(("arbitrary",)),
        name="expert_ffn",
    )(block_expert, block_valid, idx3, idx3, xn, w1, b1, w2, b2)


def _combine_kernel(idx0_ref, idxn_ref, h_ref, route_ref, fn_ref, ys_hbm, o_ref, ybuf, sem, *, tm, nt):
    i = pl.program_id(0)
    slot = i % 2
    rows = TOP_K * tm

    @pl.when(i == 0)
    def _():
        _gather_rows(idx0_ref, ys_hbm, ybuf.at[0], sem.at[0], 0, rows, DMA_UNROLL)

    @pl.when(i + 1 < nt)
    def _():
        _gather_rows(idxn_ref, ys_hbm, ybuf.at[1 - slot], sem.at[1 - slot], 0, rows, DMA_UNROLL)

    _wait_rows(ys_hbm, ybuf.at[slot], sem.at[slot], rows)
    acc = h_ref[...]
    for k in range(TOP_K):
        acc = acc + route_ref[:, TOP_K + k:TOP_K + k + 1] * _rows_to_lanes(ybuf, (slot,), k * tm, tm, F32)
    o_ref[...] = _rms(acc, fn_ref[...])


def _combine(dest_t, h, route, fn, ys):
    n = h.shape[0]
    tm = COMB_TM
    nt = n // tm
    rows = TOP_K * tm
    smem_blk = lambda fn_: pl.BlockSpec((1, 1, rows), fn_, memory_space=pltpu.SMEM)
    return pl.pallas_call(
        functools.partial(_combine_kernel, tm=tm, nt=nt),
        grid=(nt,),
        in_specs=[smem_blk(lambda i: (0, 0, 0)), smem_blk(lambda i: (jnp.minimum(i + 1, nt - 1), 0, 0)),
                  pl.BlockSpec((tm, D_MODEL), lambda i: (i, 0)), pl.BlockSpec((tm, LANES), lambda i: (i, 0)),
                  pl.BlockSpec((1, D_MODEL), lambda i: (0, 0)), pl.BlockSpec(memory_space=pl.ANY)],
        out_specs=pl.BlockSpec((tm, D_MODEL), lambda i: (i, 0)),
        out_shape=jax.ShapeDtypeStruct((n, D_MODEL), F32),
        scratch_shapes=[pltpu.VMEM((2, rows * ROW_CHUNKS, LANES), F32), pltpu.SemaphoreType.DMA((2,))],
        compiler_params=_cparams(("arbitrary",)),
        name="combine",
    )(dest_t, dest_t, h, route, fn, ys)


def _rope_tables(positions, rot, lead, period):
    inv_freq = ROPE_THETA ** (-jnp.arange(0, rot, 2, dtype=F32) / rot)
    ang = positions.astype(F32)[..., None] * inv_freq
    cos, sin = jnp.cos(ang), jnp.sin(ang)
    shape = cos.shape[:-1]
    tail = period - lead - rot
    c = jnp.concatenate([jnp.ones(shape + (lead,), F32), cos, cos, jnp.ones(shape + (tail,), F32)], axis=-1)
    s = jnp.concatenate([jnp.zeros(shape + (lead,), F32), -sin, sin, jnp.zeros(shape + (tail,), F32)], axis=-1)
    reps = LANES // period
    return jnp.tile(c, reps), jnp.tile(s, reps)


def _layout_w_in(w):
    d = w.shape[0]
    sizes = (MLA_Q_RANK, MLA_KV_RANK, MLA_ROPE_DIM, NSA_HEADS * NSA_HEAD_DIM) + (NSA_KV_HEADS * NSA_HEAD_DIM,) * 6 \
        + (3 * NSA_HEADS,)
    offs = [0]
    for sz in sizes:
        offs.append(offs[-1] + sz)
    seg = [w[:, offs[j]:offs[j + 1]] for j in range(len(sizes))]
    z = lambda n: jnp.zeros((d, n), w.dtype)
    kpe = jnp.concatenate([z(MLA_NOPE_DIM), seg[2], z(LANES - MLA_NOPE_DIM - MLA_ROPE_DIM)], axis=1)
    gates = jnp.concatenate([seg[10], z(LANES - 3 * NSA_HEADS)], axis=1)
    out = jnp.concatenate([seg[0], seg[1], kpe, seg[3]] + seg[4:10] + [gates], axis=1)
    assert out.shape[1] == C_TOTAL
    return out


def _layout_mla_up(w_q_up, w_kv_up):
    rq = w_q_up.shape[0]
    qd = MLA_NOPE_DIM + MLA_ROPE_DIM
    wq = w_q_up.reshape(rq, MLA_HEADS, qd)
    wq = jnp.pad(wq, ((0, 0), (0, 0), (0, LANES - qd))).reshape(rq, MLA_HEADS * LANES)
    rk = w_kv_up.shape[0]
    wkv = w_kv_up.reshape(rk, MLA_HEADS, MLA_NOPE_DIM + MLA_V_DIM)
    wk = jnp.pad(wkv[:, :, :MLA_NOPE_DIM], ((0, 0), (0, 0), (0, LANES - MLA_NOPE_DIM))).reshape(rk, MLA_HEADS * LANES)
    wv = wkv[:, :, MLA_NOPE_DIM:].reshape(rk, MLA_HEADS * MLA_V_DIM)
    return wq, wk, wv


def _layout_compress(cmp_pos, cmp_w1, cmp_w2):
    g = NSA_KV_HEADS
    eye = jnp.eye(g, dtype=F32)
    half = CMP_BLOCK // 2
    assert CMP_BLOCK == 2 * CMP_STRIDE

    def w1_part(w):
        return jnp.einsum('ldh,gk->lgdkh', w, eye).reshape(half * g * NSA_HEAD_DIM, g * CMP_HIDDEN)

    def pos_part(p):
        return jnp.broadcast_to(p[:, None, :], (half, g, NSA_HEAD_DIM)).reshape(1, -1)

    w1a = jnp.stack([w1_part(cmp_w1[i, :half]) for i in range(2)]).astype(BF16)
    w1b = jnp.stack([w1_part(cmp_w1[i, half:]) for i in range(2)]).astype(BF16)
    w2 = jnp.stack([jnp.einsum('hd,gk->ghkd', cmp_w2[i], eye).reshape(g * CMP_HIDDEN, g * NSA_HEAD_DIM)
                    for i in range(2)]).astype(BF16)
    pos = jnp.stack([jnp.concatenate([pos_part(cmp_pos[i, :half]), pos_part(cmp_pos[i, half:])], axis=0)
                     for i in range(2)])
    return pos, w1a, w1b, w2


def _layout_experts(w1, b1, w2):
    e, d, _ = w1.shape
    fc = D_FF // FFN_CHUNKS

    def chunked(a):
        lead = a.shape[:-1]
        gate, up = a[..., :D_FF].reshape(lead + (FFN_CHUNKS, fc)), a[..., D_FF:].reshape(lead + (FFN_CHUNKS, fc))
        return jnp.concatenate([gate, up], axis=-1)

    w1c = chunked(w1).transpose(0, 2, 1, 3).astype(BF16)
    b1c = chunked(b1)[:, :, None, :]
    w2c = w2.reshape(e, FFN_CHUNKS, fc, w2.shape[-1]).astype(BF16)
    return w1c, b1c, w2c


def _overlap_matrix(rows, n_cmp):
    n = jnp.arange(rows)[:, None]
    j = jnp.arange(LANES)[None, :] % HALF
    start = n * CMP_STRIDE
    ov = (start < (j + 1) * SEL_BLOCK) & (start + CMP_BLOCK > j * SEL_BLOCK) & (n < n_cmp)
    return ov.astype(BF16)


def _gate_expand_matrix():
    width = NSA_HEADS * NSA_HEAD_DIM
    rows = jnp.arange(LANES)[:, None]
    cols = jnp.arange(3 * width)[None, :]
    branch, head = cols // width, (cols % width) // NSA_HEAD_DIM
    return (rows == head * 3 + branch).astype(BF16)


def kernel(x, positions, attn_norm, w_in, mla_q_norm, mla_w_q_up, mla_kv_norm, mla_w_kv_up, nsa_cmp_pos,
           nsa_cmp_w1, nsa_cmp_w2, w_out, ffn_norm, router_w, router_b, moe_w1, moe_b1, moe_w2, moe_b2, final_norm):
    b, s, d = x.shape
    n = b * s
    assert attn_norm.shape[0] == 1, "single-layer configuration"
    assert d == D_MODEL and s % SEL_BLOCK == 0 and s // SEL_BLOCK <= HALF and s % ATT_T == 0
    assert s % PROJ_TM == 0 and ATT_T % PROJ_TM == 0 and n % POST_TM == 0 and n % COMB_TM == 0

    mla_c, mla_s = _rope_tables(positions.reshape(n), MLA_ROPE_DIM, MLA_NOPE_DIM, LANES)
    nsa_c, nsa_s = _rope_tables(positions.reshape(n), NSA_ROPE_DIM, 0, NSA_HEAD_DIM)
    rows = s // CMP_STRIDE
    n_cmp = (s - CMP_BLOCK) // CMP_STRIDE + 1
    cmp_end = jnp.minimum(jnp.arange(rows) * CMP_STRIDE + CMP_BLOCK - 1, s - 1)
    cmp_c, cmp_s = _rope_tables(positions[:, cmp_end], NSA_ROPE_DIM, 0, NSA_HEAD_DIM)
    ov = _overlap_matrix(rows, n_cmp)
    eg = _gate_expand_matrix()

    x2 = x.reshape(n, d)
    win = _layout_w_in(w_in[0]).astype(BF16)
    wq, wk, wv = (w.astype(BF16) for w in _layout_mla_up(mla_w_q_up[0], mla_w_kv_up[0]))
    (qm, km, vmt, qs, kc_in, vc_in, ksx, vst, kwx, vwt, gl) = _proj(
        x2, (mla_c, mla_s, nsa_c, nsa_s), attn_norm[0][None], win, mla_q_norm[0][None], wq,
        mla_kv_norm[0][None], wk, wv, b, s)

    pos, w1a, w1b, w2c = _layout_compress(nsa_cmp_pos[0], nsa_cmp_w1[0], nsa_cmp_w2[0])
    flat = CMP_STRIDE * NSA_KV_HEADS * NSA_HEAD_DIM
    kcx, vcx = _compress(kc_in.reshape(b, rows, flat), vc_in.reshape(b, rows, flat), pos, w1a, w1b, w2c,
                         cmp_c, cmp_s)

    o_mla = _mla_attention(qm.reshape(b, s, -1), km.reshape(b, s, -1), vmt)
    o_cmp, qa = _cmp_attention(qs.reshape(b, s, -1), kcx, vcx, ov)
    o_sel, o_win = _selwin_attention(qa, ksx.reshape(b, s, -1), vst, kwx.reshape(b, s, -1), vwt)

    rw = jnp.pad(router_w[0], ((0, 0), (0, LANES - N_EXPERTS)))
    rw_hi = rw.astype(BF16)
    rw_lo = (rw - rw_hi.astype(F32)).astype(BF16)
    rb = jnp.pad(router_b[0], (0, LANES - N_EXPERTS))[None]
    hres, xn, route, counts = _post(
        x2, o_mla.reshape(n, -1), o_cmp.reshape(n, -1), o_sel.reshape(n, -1), o_win.reshape(n, -1), gl, eg,
        w_out[0].astype(BF16), ffn_norm[0][None], rw_hi, rw_lo, rb)

    top_idx = route[:, 0:TOP_K].astype(I32)
    rank = route[:, 2 * TOP_K:3 * TOP_K].astype(I32)
    cnt = counts[0, :N_EXPERTS].astype(I32)
    padded = ((cnt + FFN_BM - 1) // FFN_BM) * FFN_BM
    pend = jnp.cumsum(padded)
    pstart = pend - padded
    dest = pstart[top_idx] + rank
    nb = -(-(n * TOP_K) // FFN_BM) + N_EXPERTS + 1
    block_row = jnp.arange(nb, dtype=I32) * FFN_BM
    block_expert = jnp.minimum(jnp.sum((block_row[:, None] >= pend[None, :]).astype(I32), axis=1), N_EXPERTS - 1)
    block_valid = (block_row < pend[-1]).astype(I32)
    tok_ids = jnp.broadcast_to(jnp.arange(n, dtype=I32)[:, None], (n, TOP_K))
    row_tok = jnp.zeros((nb * FFN_BM,), I32).at[dest.reshape(-1)].set(tok_ids.reshape(-1), unique_indices=True)

    w1c, b1c, w2c_ = _layout_experts(moe_w1[0], moe_b1[0], moe_w2[0])
    ys = _ffn(block_expert, block_valid, row_tok, xn, w1c, b1c, w2c_, moe_b2[0][:, None, :])
    dest_t = dest.reshape(n // COMB_TM, COMB_TM, TOP_K).transpose(0, 2, 1).reshape(n // COMB_TM, 1, TOP_K * COMB_TM)
    out = _combine(dest_t, hres, route, final_norm[None], ys)
    return out.reshape(b, s, d)
```

```python
import functools
import math

import jax
import jax.numpy as jnp
from jax import lax
from jax.experimental import pallas as pl
from jax.experimental.pallas import tpu as pltpu

F32 = jnp.float32
BF16 = jnp.bfloat16
I32 = jnp.int32

D_MODEL = 1024
ROPE_THETA = 500000.0
NORM_EPS = 1e-5
NEG_INF = -1e30
POS_INF = 1e30
LOG2E = math.log2(math.e)

MLA_HEADS = 8
MLA_NOPE_DIM = 64
MLA_ROPE_DIM = 32
MLA_V_DIM = 64
MLA_Q_RANK = 256
MLA_KV_RANK = 128

NSA_HEADS = 8
NSA_KV_HEADS = 2
NSA_GROUP = NSA_HEADS // NSA_KV_HEADS
NSA_HEAD_DIM = 64
NSA_ROPE_DIM = NSA_HEAD_DIM // 4
CMP_BLOCK = 32
CMP_STRIDE = 16
CMP_HIDDEN = 2 * NSA_HEAD_DIM
SEL_BLOCK = 64
N_SEL = 16
N_LOCAL_SEL = 2
WINDOW = 512

N_EXPERTS = 32
TOP_K = 4
D_FF = D_MODEL
SWIGLU_LIMIT = 7.0
SWIGLU_ALPHA = 1.702

LANES = 128
HALF = LANES // 2
VMEM_LIMIT = 48 * 1024 * 1024

PROJ_TM = 256
ATT_T = 512
ATT_QS = 256
ATT_KC_FULL = 512
ATT_KC_DIAG = 256
ATT_AHEAD = 4
MLA_TILES = 4
SEL_TILES = 2
V_PAD = 16
CMP_TQ = 256
POST_TM = 256
FFN_BM = 512
COMB_TM = 128
FFN_AHEAD = 2
FFN_SLOTS = FFN_AHEAD + 1
DMA_UNROLL = 8
ROW_CHUNKS = D_MODEL // LANES

C_QLAT = 0
C_KVLAT = C_QLAT + MLA_Q_RANK
C_KPE = C_KVLAT + MLA_KV_RANK
C_QNSA = C_KPE + LANES
C_KC = C_QNSA + NSA_HEADS * NSA_HEAD_DIM
C_VC = C_KC + LANES
C_KS = C_VC + LANES
C_VS = C_KS + LANES
C_KW = C_VS + LANES
C_VW = C_KW + LANES
C_GATE = C_VW + LANES
C_TOTAL = C_GATE + LANES


def _cparams(sem):
    return pltpu.CompilerParams(dimension_semantics=sem, vmem_limit_bytes=VMEM_LIMIT)


def _rms(x, g):
    return x * lax.rsqrt(jnp.mean(x * x, axis=-1, keepdims=True) + NORM_EPS) * g


def _sigmoid(x):
    return 1.0 / (1.0 + jnp.exp(-x))


def _rope(x, c, s, half, first):
    n = x.shape[-1]
    partner = jnp.where(first, pltpu.roll(x, n - half, 1), pltpu.roll(x, half, 1))
    return x * c + partner * s


def _expand_pair(x, lo, fill):
    y = pltpu.roll(x, HALF, 1)
    return (jnp.where(lo, x, fill), jnp.where(lo, fill, y), jnp.where(lo, y, fill), jnp.where(lo, fill, x))


def _proj_kernel(x_ref, mc_ref, ms_ref, nc_ref, ns_ref, an_ref, win_ref, qn_ref, wq_ref, kvn_ref, wk_ref,
                 wv_ref, qm_ref, km_ref, vm_ref, qs_ref, kc_ref, vc_ref, ks_ref, vs_ref, kw_ref, vw_ref,
                 g_ref, *, tm, seq):
    x = x_ref[...]
    u = _rms(x, an_ref[...])
    y = jnp.dot(u.astype(BF16), win_ref[...], preferred_element_type=F32)
    lane = lax.broadcasted_iota(I32, (1, LANES), 1)
    lo = lane < HALF
    mla_first = lane < MLA_NOPE_DIM + MLA_ROPE_DIM // 2
    nsa_first = (lane % NSA_HEAD_DIM) < NSA_ROPE_DIM // 2
    mc, ms, nc, ns = mc_ref[...], ms_ref[...], nc_ref[...], ns_ref[...]

    qn = _rms(y[:, C_QLAT:C_QLAT + MLA_Q_RANK], qn_ref[...])
    q = jnp.dot(qn.astype(BF16), wq_ref[...], preferred_element_type=F32)
    q = q * ((MLA_NOPE_DIM + MLA_ROPE_DIM) ** -0.5 * LOG2E)
    for h in range(MLA_HEADS):
        sl = slice(h * LANES, (h + 1) * LANES)
        qm_ref[:, sl] = _rope(q[:, sl], mc, ms, MLA_ROPE_DIM // 2, mla_first).astype(BF16)
    kvn = _rms(y[:, C_KVLAT:C_KVLAT + MLA_KV_RANK], kvn_ref[...]).astype(BF16)
    kpe = _rope(y[:, C_KPE:C_KPE + LANES], mc, ms, MLA_ROPE_DIM // 2, mla_first)
    kn = jnp.dot(kvn, wk_ref[...], preferred_element_type=F32)
    for h in range(MLA_HEADS):
        sl = slice(h * LANES, (h + 1) * LANES)
        km_ref[:, sl] = (kn[:, sl] + kpe).astype(BF16)
    ones_pad = (lax.broadcasted_iota(I32, (V_PAD, tm), 0) == 0).astype(BF16)
    vmt = jnp.dot(kvn, wv_ref[...], preferred_element_type=F32).T.astype(BF16)
    for h in range(MLA_HEADS):
        base = h * (MLA_V_DIM + V_PAD)
        vm_ref[0, 0, base:base + MLA_V_DIM, :] = vmt[h * MLA_V_DIM:(h + 1) * MLA_V_DIM]
        vm_ref[0, 0, base + MLA_V_DIM:base + MLA_V_DIM + V_PAD, :] = ones_pad

    for c in range(NSA_HEADS * NSA_HEAD_DIM // LANES):
        ch = y[:, C_QNSA + c * LANES:C_QNSA + (c + 1) * LANES]
        qs_ref[:, c * LANES:(c + 1) * LANES] = (
            _rope(ch, nc, ns, NSA_ROPE_DIM // 2, nsa_first) * (NSA_HEAD_DIM ** -0.5 * LOG2E)).astype(BF16)
    kc_ref[...] = y[:, C_KC:C_KC + LANES]
    vc_ref[...] = y[:, C_VC:C_VC + LANES]
    row = lax.broadcasted_iota(I32, (tm, LANES), 0)
    tok = (pl.program_id(0) * tm + row) % seq
    onehot = ((lane % HALF) == tok // SEL_BLOCK).astype(F32)
    ks = _rope(y[:, C_KS:C_KS + LANES], nc, ns, NSA_ROPE_DIM // 2, nsa_first)
    kw = _rope(y[:, C_KW:C_KW + LANES], nc, ns, NSA_ROPE_DIM // 2, nsa_first)
    for ref, val, fill in ((ks_ref, ks, onehot), (kw_ref, kw, 0.0)):
        for c, chunk in enumerate(_expand_pair(val, lo, fill)):
            ref[:, c * LANES:(c + 1) * LANES] = chunk.astype(BF16)
    for ref, col in ((vs_ref, C_VS), (vw_ref, C_VW)):
        vt = y[:, col:col + LANES].T.astype(BF16)
        for g in range(NSA_KV_HEADS):
            base = g * (NSA_HEAD_DIM + V_PAD)
            ref[0, 0, base:base + NSA_HEAD_DIM, :] = vt[g * NSA_HEAD_DIM:(g + 1) * NSA_HEAD_DIM]
            ref[0, 0, base + NSA_HEAD_DIM:base + NSA_HEAD_DIM + V_PAD, :] = ones_pad
    g_ref[...] = y[:, C_GATE:C_GATE + LANES]


def _proj(x2, tabs, an, win, qn, wq, kvn, wk, wv, batch, seq):
    n = x2.shape[0]
    tm = PROJ_TM
    per_b, per_slab = seq // tm, ATT_T // tm
    tok = lambda w: pl.BlockSpec((tm, w), lambda i: (i, 0))
    tr = lambda r: pl.BlockSpec((1, 1, r, tm),
                                lambda i: (i // per_b, (i % per_b) // per_slab, 0, (i % per_b) % per_slab))
    full = lambda a: pl.BlockSpec(a.shape, lambda i: (0,) * a.ndim)
    tok_out = lambda w, d: (tok(w), jax.ShapeDtypeStruct((n, w), d))
    tr_out = lambda r: (tr(r), jax.ShapeDtypeStruct((batch, seq // ATT_T, r, ATT_T), BF16))
    outs = [tok_out(8 * LANES, BF16), tok_out(8 * LANES, BF16), tr_out(MLA_HEADS * (MLA_V_DIM + V_PAD)),
            tok_out(4 * LANES, BF16), tok_out(LANES, F32), tok_out(LANES, F32),
            tok_out(4 * LANES, BF16), tr_out(NSA_KV_HEADS * (NSA_HEAD_DIM + V_PAD)), tok_out(4 * LANES, BF16),
            tr_out(NSA_KV_HEADS * (NSA_HEAD_DIM + V_PAD)), tok_out(LANES, F32)]
    return pl.pallas_call(
        functools.partial(_proj_kernel, tm=tm, seq=seq),
        grid=(n // tm,),
        in_specs=[tok(D_MODEL)] + [tok(LANES)] * 4 + [full(a) for a in (an, win, qn, wq, kvn, wk, wv)],
        out_specs=[o[0] for o in outs],
        out_shape=[o[1] for o in outs],
        compiler_params=_cparams(("parallel",)),
        name="proj",
    )(x2, *tabs, an, win, qn, wq, kvn, wk, wv)


def _compress_kernel(kin_ref, vin_ref, pos_ref, w1a_ref, w1b_ref, w2_ref, cc_ref, cs_ref, kc_ref, vc_ref, *, rows):
    lane = lax.broadcasted_iota(I32, (1, LANES), 1)
    lo = lane < HALF
    nsa_first = (lane % NSA_HEAD_DIM) < NSA_ROPE_DIM // 2

    def comp(x, i):
        p = jnp.dot((x + pos_ref[i, 0:1, :]).astype(BF16), w1a_ref[i], preferred_element_type=F32)
        q = jnp.dot((x + pos_ref[i, 1:2, :]).astype(BF16), w1b_ref[i], preferred_element_type=F32)
        hid = p + pltpu.roll(q, rows - 1, 0)
        hid = hid * _sigmoid(hid)
        return jnp.dot(hid.astype(BF16), w2_ref[i], preferred_element_type=F32)

    kc = _rope(comp(kin_ref[0], 0), cc_ref[0], cs_ref[0], NSA_ROPE_DIM // 2, nsa_first)
    vc = comp(vin_ref[0], 1)
    for ref, val in ((kc_ref, kc), (vc_ref, vc)):
        for c, chunk in enumerate(_expand_pair(val, lo, 0.0)):
            ref[0, :, c * LANES:(c + 1) * LANES] = chunk.astype(BF16)


def _compress(kin, vin, pos, w1a, w1b, w2, cc, cs):
    b, rows, width = kin.shape
    per_b = lambda w: pl.BlockSpec((1, rows, w), lambda i: (i, 0, 0))
    full = lambda a: pl.BlockSpec(a.shape, lambda i: (0,) * a.ndim)
    return pl.pallas_call(
        functools.partial(_compress_kernel, rows=rows),
        grid=(b,),
        in_specs=[per_b(width), per_b(width), full(pos), full(w1a), full(w1b), full(w2), per_b(LANES), per_b(LANES)],
        out_specs=[per_b(4 * LANES), per_b(4 * LANES)],
        out_shape=[jax.ShapeDtypeStruct((b, rows, 4 * LANES), BF16)] * 2,
        compiler_params=_cparams(("parallel",)),
        name="compress",
    )(kin, vin, pos, w1a, w1b, w2, cc, cs)


def _subtile_plan(t, kc, delta, window):
    plan = []
    for j in range(t // ATT_QS):
        for c in range(t // kc):
            if delta is None:
                plan.append((j, c, False))
                continue
            dmin = delta + j * ATT_QS - (c * kc + kc - 1)
            dmax = delta + j * ATT_QS + ATT_QS - 1 - c * kc
            if dmax < 0 or (window is not None and dmin >= window):
                continue
            full = dmin >= 0 and (window is None or dmax < window)
            plan.append((j, c, not full))
    return plan


def _flash_tiles(q_of, heads, t, tiles):
    jsl = lambda j: slice(j * ATT_QS, (j + 1) * ATT_QS)
    items = []
    for tile in tiles:
        plan = _subtile_plan(t, tile["kc"], tile["delta"], tile["window"])
        for h in heads:
            for j in sorted({jj for jj, _, _ in plan}):
                items += [(tile, h, j, c, masked) for jj, c, masked in plan if jj == j]
    keys = []
    for tile, h, j, _, _ in items:
        key = (id(tile["m_ref"]), h, j)
        if key not in [k for k, _ in keys]:
            keys.append((key, (tile["m_ref"], tile["acc_ref"], h, j)))
    state = {key: (m_ref[h:h + 1, jsl(j)], acc_ref[h, :, jsl(j)]) for key, (m_ref, acc_ref, h, j) in keys}

    def scores(item):
        tile, h, j, c, masked = item
        kc = tile["kc"]
        st = lax.dot_general(tile["k_of"](h, c, kc), q_of(h, j), (((1,), (1,)), ((), ())),
                             preferred_element_type=F32)
        if masked:
            kk = lax.broadcasted_iota(I32, (kc, ATT_QS), 0)
            qq = lax.broadcasted_iota(I32, (kc, ATT_QS), 1)
            d = (tile["delta"] + j * ATT_QS - c * kc) + qq - kk
            ok = d >= 0
            if tile["window"] is not None:
                ok = ok & (d < tile["window"])
            st = jnp.where(ok, st, NEG_INF)
        return st

    def fold(pending):
        key, a_prev, pv_prev = pending
        mm, aa = state[key]
        state[key] = (mm, aa * a_prev + pv_prev)

    ahead = [scores(it) for it in items[:ATT_AHEAD]]
    pending = None
    for n, (tile, h, j, c, _) in enumerate(items):
        st = ahead.pop(0)
        if n + ATT_AHEAD < len(items):
            ahead.append(scores(items[n + ATT_AHEAD]))
        key = (id(tile["m_ref"]), h, j)
        m = state[key][0]
        m_new = jnp.maximum(m, jnp.max(st, axis=0, keepdims=True))
        alpha = jnp.exp2(m - m_new)
        p = jnp.exp2((st - m_new).astype(BF16))
        pv = jnp.dot(tile["vt_of"](h, c, tile["kc"]), p, preferred_element_type=F32)
        if pending is not None:
            fold(pending)
        state[key] = (m_new, state[key][1])
        pending = (key, alpha, pv)
    fold(pending)
    for key, (m_ref, acc_ref, h, j) in keys:
        m, acc = state[key]
        m_ref[h:h + 1, jsl(j)] = m
        acc_ref[h, :, jsl(j)] = acc


def _flash_init(m_ref, acc_ref):
    m_ref[...] = jnp.full(m_ref.shape, NEG_INF, F32)
    acc_ref[...] = jnp.zeros(acc_ref.shape, F32)


def _flash_out_pair(acc_ref, pair, dv):
    parts = [acc_ref[h, 0:dv, :] / acc_ref[h, dv:dv + 1, :] for h in (2 * pair, 2 * pair + 1)]
    return jnp.concatenate(parts, axis=0).T


def _kv_readers(k_ref, vt_ref, kt, t, k_chunk_of_head, v_rows_of_head):
    def k_of(h, c, kc):
        start = pl.multiple_of(kt * t + c * kc, kc)
        return k_ref[0, pl.ds(start, kc), k_chunk_of_head(h)]

    def vt_of(h, c, kc):
        return vt_ref[0, kt, v_rows_of_head(h), c * kc:(c + 1) * kc]

    return k_of, vt_of


def _mla_kernel(q_ref, k_ref, vt_ref, o_ref, m_ref, acc_ref, *, t):
    qi = pl.program_id(2)
    heads = (0, 1)
    chunk = lambda h: slice(h * LANES, (h + 1) * LANES)
    vrows = lambda h: slice(h * (MLA_V_DIM + V_PAD), (h + 1) * (MLA_V_DIM + V_PAD))
    q_of = lambda h, j: q_ref[0, j * ATT_QS:(j + 1) * ATT_QS, chunk(h)]
    _flash_init(m_ref, acc_ref)

    def tile(kt, kc, delta):
        k_of, vt_of = _kv_readers(k_ref, vt_ref, kt, t, chunk, vrows)
        return dict(k_of=k_of, vt_of=vt_of, kc=kc, delta=delta, window=None, m_ref=m_ref, acc_ref=acc_ref)

    def body(kp, carry):
        _flash_tiles(q_of, heads, t, [tile(MLA_TILES * kp + u, ATT_KC_FULL, None) for u in range(MLA_TILES)])
        return carry

    lax.fori_loop(0, qi // MLA_TILES, body, 0)
    for rem in range(MLA_TILES):
        @pl.when(qi % MLA_TILES == rem)
        def _(rem=rem):
            _flash_tiles(q_of, heads, t, [tile(qi - rem + u, ATT_KC_FULL, None) for u in range(rem)]
                         + [tile(qi, ATT_KC_DIAG, 0)])

    o_ref[0] = _flash_out_pair(acc_ref, 0, MLA_V_DIM)


def _mla_attention(q, k, vt):
    b, s, _ = q.shape
    t = ATT_T
    pairs = MLA_HEADS // 2
    return pl.pallas_call(
        functools.partial(_mla_kernel, t=t),
        grid=(b, pairs, s // t),
        in_specs=[pl.BlockSpec((1, t, 2 * LANES), lambda bi, p, qi: (bi, qi, p)),
                  pl.BlockSpec((1, s, 2 * LANES), lambda bi, p, qi: (bi, 0, p)),
                  pl.BlockSpec((1, s // t, 2 * (MLA_V_DIM + V_PAD), t), lambda bi, p, qi: (bi, 0, p, 0))],
        out_specs=pl.BlockSpec((1, t, LANES), lambda bi, p, qi: (bi, qi, p)),
        out_shape=jax.ShapeDtypeStruct((b, s, pairs * LANES), F32),
        scratch_shapes=[pltpu.VMEM((2, t), F32), pltpu.VMEM((2, MLA_V_DIM + V_PAD, t), F32)],
        compiler_params=_cparams(("parallel", "parallel", "arbitrary")),
        name="mla_attention",
    )(q, k, vt)


def _cmp_kernel(q_ref, kc_ref, vc_ref, ov_ref, o_ref, qa_ref, *, tq, rows):
    qi = pl.program_id(2)
    lane = lax.broadcasted_iota(I32, (1, LANES), 1)
    lo = lane < HALF
    t_pos = qi * tq + lax.broadcasted_iota(I32, (tq, 1), 0)
    cmp_end = lax.broadcasted_iota(I32, (1, rows), 1) * CMP_STRIDE + (CMP_BLOCK - 1)
    mask = cmp_end <= t_pos
    live = (t_pos >= CMP_BLOCK - 1).astype(F32)

    psum = jnp.zeros((tq, rows), F32)
    for pair in range(NSA_GROUP // 2):
        qp = q_ref[0, :, pair * LANES:(pair + 1) * LANES]
        acc = jnp.zeros((tq, LANES), F32)
        for e in range(2):
            sl = slice(e * LANES, (e + 1) * LANES)
            s = lax.dot_general(qp, kc_ref[0, :, sl], (((1,), (1,)), ((), ())), preferred_element_type=F32)
            s = jnp.where(mask, s, NEG_INF)
            p = jnp.exp2(s - jnp.max(s, axis=-1, keepdims=True))
            p = p / jnp.sum(p, axis=-1, keepdims=True) * live
            psum = psum + p
            acc = acc + jnp.dot(p.astype(BF16), vc_ref[0, :, sl], preferred_element_type=F32)
        o_ref[0, :, pair * LANES:(pair + 1) * LANES] = acc

    p_hi = psum.astype(BF16)
    p_lo = (psum - p_hi.astype(F32)).astype(BF16)
    imp = (jnp.dot(p_hi, ov_ref[...], preferred_element_type=F32)
           + jnp.dot(p_lo, ov_ref[...], preferred_element_type=F32))
    blk = lane % HALF
    cur = t_pos // SEL_BLOCK
    forced = (blk == 0) | ((blk <= cur) & (blk > cur - N_LOCAL_SEL))
    val = jnp.where(forced, POS_INF, jnp.where(blk <= cur, imp, NEG_INF))

    vt = val.T[:HALF]
    sub = 8
    groups = [vt[a * sub:(a + 1) * sub] for a in range(HALF // sub)]
    cnts = [jnp.zeros((sub, tq), I32) for _ in groups]
    jsub = lax.broadcasted_iota(I32, (sub, tq), 0)
    for i in range(HALF):
        r = vt[i:i + 1, :]
        for a, grp in enumerate(groups):
            if a > i // sub:
                ahead = r >= grp
            elif a < i // sub:
                ahead = r > grp
            else:
                ahead = (r > grp) | ((r == grp) & (i % sub < jsub))
            cnts[a] = cnts[a] + ahead.astype(I32)
    cnt = jnp.concatenate(cnts, axis=0)
    bias_t = jnp.where(cnt < N_SEL, 0.0, NEG_INF).astype(F32)
    bias = jnp.concatenate([bias_t, bias_t], axis=0).T.astype(BF16)
    for h in range(NSA_GROUP):
        qp = q_ref[0, :, (h // 2) * LANES:(h // 2 + 1) * LANES]
        chunk = jnp.where(lo, qp, bias) if h % 2 == 0 else jnp.where(lo, bias, qp)
        qa_ref[0, :, h * LANES:(h + 1) * LANES] = chunk


def _cmp_attention(q, kcx, vcx, ov):
    b, s, _ = q.shape
    rows = kcx.shape[1]
    tq = min(CMP_TQ, s)
    g = NSA_KV_HEADS
    return pl.pallas_call(
        functools.partial(_cmp_kernel, tq=tq, rows=rows),
        grid=(b, g, s // tq),
        in_specs=[pl.BlockSpec((1, tq, 2 * LANES), lambda bi, gi, qi: (bi, qi, gi)),
                  pl.BlockSpec((1, rows, 2 * LANES), lambda bi, gi, qi: (bi, 0, gi)),
                  pl.BlockSpec((1, rows, 2 * LANES), lambda bi, gi, qi: (bi, 0, gi)),
                  pl.BlockSpec(ov.shape, lambda bi, gi, qi: (0, 0))],
        out_specs=[pl.BlockSpec((1, tq, 2 * LANES), lambda bi, gi, qi: (bi, qi, gi)),
                   pl.BlockSpec((1, tq, 4 * LANES), lambda bi, gi, qi: (bi, qi, gi))],
        out_shape=[jax.ShapeDtypeStruct((b, s, g * 2 * LANES), F32),
                   jax.ShapeDtypeStruct((b, s, g * 4 * LANES), BF16)],
        compiler_params=_cparams(("parallel", "parallel", "parallel")),
        name="cmp_attention",
    )(q, kcx, vcx, ov)


def _selwin_kernel(qa_ref, ks_ref, vs_ref, kw_ref, vw_ref, os_ref, ow_ref, ms_ref, as_ref, mw_ref, aw_ref, *, t):
    qi = pl.program_id(2)
    heads = tuple(range(NSA_GROUP))
    chunk = lambda h: slice((h % 2) * LANES, (h % 2 + 1) * LANES)
    vrows = lambda h: slice(0, NSA_HEAD_DIM + V_PAD)
    q_of = lambda h, j: qa_ref[0, j * ATT_QS:(j + 1) * ATT_QS, h * LANES:(h + 1) * LANES]
    _flash_init(ms_ref, as_ref)
    _flash_init(mw_ref, aw_ref)

    def sel_tile(kt, kc, delta):
        k_of, vt_of = _kv_readers(ks_ref, vs_ref, kt, t, chunk, vrows)
        return dict(k_of=k_of, vt_of=vt_of, kc=kc, delta=delta, window=None, m_ref=ms_ref, acc_ref=as_ref)

    def win_tile(kt, delta):
        k_of, vt_of = _kv_readers(kw_ref, vw_ref, kt, t, chunk, vrows)
        return dict(k_of=k_of, vt_of=vt_of, kc=ATT_KC_DIAG, delta=delta, window=WINDOW, m_ref=mw_ref, acc_ref=aw_ref)

    def body(kp, carry):
        _flash_tiles(q_of, heads, t, [sel_tile(SEL_TILES * kp + u, ATT_KC_FULL, None) for u in range(SEL_TILES)])
        return carry

    lax.fori_loop(0, qi // SEL_TILES, body, 0)

    for rem in range(SEL_TILES):
        @pl.when((qi % SEL_TILES == rem) & (qi > 0))
        def _(rem=rem):
            _flash_tiles(q_of, heads, t, [sel_tile(qi - rem + u, ATT_KC_FULL, None) for u in range(rem)]
                         + [win_tile(qi - 1, t), sel_tile(qi, ATT_KC_DIAG, 0), win_tile(qi, 0)])

    @pl.when(qi == 0)
    def _():
        _flash_tiles(q_of, heads, t, [sel_tile(qi, ATT_KC_DIAG, 0), win_tile(qi, 0)])

    for pair in range(NSA_GROUP // 2):
        sl = slice(pair * LANES, (pair + 1) * LANES)
        os_ref[0, :, sl] = _flash_out_pair(as_ref, pair, NSA_HEAD_DIM)
        ow_ref[0, :, sl] = _flash_out_pair(aw_ref, pair, NSA_HEAD_DIM)


def _selwin_attention(qa, ksx, vst, kwx, vwt):
    b, s, _ = qa.shape
    t = ATT_T
    assert t >= WINDOW, "window branch reads only the previous and the diagonal key tile"
    g = NSA_KV_HEADS
    pairs = NSA_GROUP // 2
    kspec = pl.BlockSpec((1, s, 2 * LANES), lambda bi, gi, qi: (bi, 0, gi))
    vspec = pl.BlockSpec((1, s // t, NSA_HEAD_DIM + V_PAD, t), lambda bi, gi, qi: (bi, 0, gi, 0))
    ospec = pl.BlockSpec((1, t, pairs * LANES), lambda bi, gi, qi: (bi, qi, gi))
    stat = lambda: pltpu.VMEM((NSA_GROUP, t), F32)
    accs = lambda: pltpu.VMEM((NSA_GROUP, NSA_HEAD_DIM + V_PAD, t), F32)
    return pl.pallas_call(
        functools.partial(_selwin_kernel, t=t),
        grid=(b, g, s // t),
        in_specs=[pl.BlockSpec((1, t, NSA_GROUP * LANES), lambda bi, gi, qi: (bi, qi, gi)),
                  kspec, vspec, kspec, vspec],
        out_specs=[ospec, ospec],
        out_shape=[jax.ShapeDtypeStruct((b, s, g * pairs * LANES), F32)] * 2,
        scratch_shapes=[stat(), accs(), stat(), accs()],
        compiler_params=_cparams(("parallel", "parallel", "arbitrary")),
        name="selwin_attention",
    )(qa, ksx, vst, kwx, vwt)


def _post_kernel(x_ref, om_ref, oc_ref, os_ref, ow_ref, g_ref, eg_ref, wo_ref, fn_ref, rwh_ref, rwl_ref, rb_ref,
                 h_ref, xn_ref, route_ref, cnt_ref, *, tm):
    i = pl.program_id(0)
    width = NSA_HEADS * NSA_HEAD_DIM
    sg = _sigmoid(g_ref[...])
    sg_hi = sg.astype(BF16)
    sg_lo = (sg - sg_hi.astype(F32)).astype(BF16)
    ge = (jnp.dot(sg_hi, eg_ref[...], preferred_element_type=F32)
          + jnp.dot(sg_lo, eg_ref[...], preferred_element_type=F32))
    o_nsa = (ge[:, 0:width] * oc_ref[...] + ge[:, width:2 * width] * os_ref[...]
             + ge[:, 2 * width:3 * width] * ow_ref[...])
    mla_w = MLA_HEADS * MLA_V_DIM
    mixed = (jnp.dot(om_ref[...].astype(BF16), wo_ref[0:mla_w, :], preferred_element_type=F32)
             + jnp.dot(o_nsa.astype(BF16), wo_ref[mla_w:mla_w + width, :], preferred_element_type=F32))
    h = x_ref[...] + mixed
    h_ref[...] = h
    xn = _rms(h, fn_ref[...])
    _lanes_to_rows(xn_ref, xn)

    x_hi = xn.astype(BF16)
    x_lo = (xn - x_hi.astype(F32)).astype(BF16)
    logits = (jnp.dot(x_hi, rwh_ref[...], preferred_element_type=F32)
              + jnp.dot(x_hi, rwl_ref[...], preferred_element_type=F32)
              + jnp.dot(x_lo, rwh_ref[...], preferred_element_type=F32)) + rb_ref[...]
    lane = lax.broadcasted_iota(I32, (tm, LANES), 1)
    lg = jnp.where(lane < N_EXPERTS, logits, -jnp.inf)
    vals, hots = [], []
    for _ in range(TOP_K):
        m = jnp.max(lg, axis=-1, keepdims=True)
        idx = jnp.min(jnp.where(lg == m, lane, LANES), axis=-1, keepdims=True)
        hot = lane == idx
        lg = jnp.where(hot, -jnp.inf, lg)
        vals.append(m)
        hots.append(hot)
    es = [jnp.exp(v - vals[0]) for v in vals]
    den = es[0] + es[1] + es[2] + es[3]

    @pl.when(i == 0)
    def _():
        cnt_ref[...] = jnp.zeros(cnt_ref.shape, F32)

    hot_all = (hots[0] | hots[1] | hots[2] | hots[3]).astype(F32)
    r = lax.broadcasted_iota(I32, (tm, tm), 0)
    c = lax.broadcasted_iota(I32, (tm, tm), 1)
    tri = (c < r).astype(BF16)
    before = jnp.dot(tri, hot_all.astype(BF16), preferred_element_type=F32) + cnt_ref[...]
    route = jnp.zeros((tm, LANES), F32)
    for k in range(TOP_K):
        e_k = jnp.sum(jnp.where(hots[k], lane, 0), axis=-1, keepdims=True).astype(F32)
        rank_k = jnp.sum(jnp.where(hots[k], before, 0.0), axis=-1, keepdims=True)
        route = (route + jnp.where(lane == k, e_k, 0.0) + jnp.where(lane == TOP_K + k, es[k] / den, 0.0)
                 + jnp.where(lane == 2 * TOP_K + k, rank_k, 0.0))
    route_ref[...] = route
    cnt_ref[...] = cnt_ref[...] + jnp.sum(hot_all, axis=0, keepdims=True)


def _post(x2, om, oc, os_, ow, gl, eg, wo, fn, rwh, rwl, rb):
    n = x2.shape[0]
    tm = POST_TM
    tok = lambda w: pl.BlockSpec((tm, w), lambda i: (i, 0))
    full = lambda a: pl.BlockSpec(a.shape, lambda i: (0,) * a.ndim)
    return pl.pallas_call(
        functools.partial(_post_kernel, tm=tm),
        grid=(n // tm,),
        in_specs=[tok(D_MODEL), tok(om.shape[1]), tok(oc.shape[1]), tok(os_.shape[1]), tok(ow.shape[1]), tok(LANES)]
        + [full(a) for a in (eg, wo, fn, rwh, rwl, rb)],
        out_specs=[tok(D_MODEL), pl.BlockSpec((tm * ROW_CHUNKS, LANES), lambda i: (i, 0)), tok(LANES),
                   pl.BlockSpec((1, LANES), lambda i: (0, 0))],
        out_shape=[jax.ShapeDtypeStruct((n, D_MODEL), F32), jax.ShapeDtypeStruct((n * ROW_CHUNKS, LANES), F32),
                   jax.ShapeDtypeStruct((n, LANES), F32), jax.ShapeDtypeStruct((1, LANES), F32)],
        compiler_params=_cparams(("arbitrary",)),
        name="post",
    )(x2, om, oc, os_, ow, gl, eg, wo, fn, rwh, rwl, rb)


def _gather_rows(idx_ref, src_hbm, dst, sem, first, count, unroll):
    def body(r, carry):
        src = pl.multiple_of(idx_ref[0, 0, first + r] * ROW_CHUNKS, ROW_CHUNKS)
        dst_row = pl.multiple_of((first + r) * ROW_CHUNKS, ROW_CHUNKS)
        pltpu.make_async_copy(src_hbm.at[pl.ds(src, ROW_CHUNKS), :], dst.at[pl.ds(dst_row, ROW_CHUNKS), :], sem).start()
        return carry
    lax.fori_loop(0, count, body, 0, unroll=unroll)


def _wait_rows(src_hbm, dst, sem, count):
    pltpu.make_async_copy(src_hbm.at[pl.ds(0, count * ROW_CHUNKS), :], dst, sem).wait()


def _rows_to_lanes(ref, lead, first, rows, dtype):
    return jnp.concatenate(
        [ref[lead + (pl.ds(first * ROW_CHUNKS + c, rows, stride=ROW_CHUNKS), slice(None))].astype(dtype)
         for c in range(ROW_CHUNKS)], axis=1)


def _lanes_to_rows(ref, val):
    for c in range(ROW_CHUNKS):
        ref[pl.ds(c, val.shape[0], stride=ROW_CHUNKS), :] = val[:, c * LANES:(c + 1) * LANES]


def _ffn_kernel(be_ref, bv_ref, idx0_ref, idx1_ref, idxn_ref, xn_hbm, w1_ref, b1_ref, w2_ref, b2_ref, y_ref, xbuf, xb_ref,
                sem, *, bm):
    del be_ref
    i = pl.program_id(0)
    slot = i % FFN_SLOTS

    @pl.when(i == 0)
    def _():
        _gather_rows(idx0_ref, xn_hbm, xbuf.at[0], sem.at[0], 0, bm, DMA_UNROLL)
        _gather_rows(idx1_ref, xn_hbm, xbuf.at[1], sem.at[1], 0, bm, DMA_UNROLL)

    @pl.when(bv_ref[i] > 0)
    def _():
        _wait_rows(xn_hbm, xbuf.at[slot], sem.at[slot], bm)
        xb_ref[...] = _rows_to_lanes(xbuf, (slot,), 0, bm, BF16)
        ahead = (i + FFN_AHEAD) % FFN_SLOTS
        _gather_rows(idxn_ref, xn_hbm, xbuf.at[ahead], sem.at[ahead], 0, bm, True)
        hcat = jnp.dot(xb_ref[...], w1_ref[0], preferred_element_type=F32) + b1_ref[0]
        a = jnp.minimum(hcat[:, :D_FF], SWIGLU_LIMIT)
        up = jnp.clip(hcat[:, D_FF:], -SWIGLU_LIMIT, SWIGLU_LIMIT)
        glu = a * _sigmoid(SWIGLU_ALPHA * a)
        y = jnp.dot(((up + 1.0) * glu).astype(BF16), w2_ref[0], preferred_element_type=F32) + b2_ref[0]
        _lanes_to_rows(y_ref, y)

    @pl.when(bv_ref[i] == 0)
    def _():
        @pl.when(bv_ref[jnp.maximum(i - FFN_AHEAD, 0)] > 0)
        def _():
            _wait_rows(xn_hbm, xbuf.at[slot], sem.at[slot], bm)

        y_ref[...] = jnp.zeros(y_ref.shape, F32)


def _ffn(block_expert, block_valid, row_tok, xn, w1, b1, w2, b2):
    nb = block_expert.shape[0]
    bm = FFN_BM
    idx3 = row_tok.reshape(nb, 1, bm)
    smem_blk = lambda fn: pl.BlockSpec((1, 1, bm), fn, memory_space=pltpu.SMEM)
    grid_spec = pltpu.PrefetchScalarGridSpec(
        num_scalar_prefetch=2,
        grid=(nb,),
        in_specs=[smem_blk(lambda i, be, bv: (0, 0, 0)),
                  smem_blk(lambda i, be, bv: (1, 0, 0)),
                  smem_blk(lambda i, be, bv: (jnp.minimum(i + FFN_AHEAD, nb - 1), 0, 0)),
                  pl.BlockSpec(memory_space=pl.ANY),
                  pl.BlockSpec((1, D_MODEL, 2 * D_FF), lambda i, be, bv: (be[i], 0, 0)),
                  pl.BlockSpec((1, 1, 2 * D_FF), lambda i, be, bv: (be[i], 0, 0)),
                  pl.BlockSpec((1, D_FF, D_MODEL), lambda i, be, bv: (be[i], 0, 0)),
                  pl.BlockSpec((1, 1, D_MODEL), lambda i, be, bv: (be[i], 0, 0))],
        out_specs=pl.BlockSpec((bm * ROW_CHUNKS, LANES), lambda i, be, bv: (i, 0)),
        scratch_shapes=[pltpu.VMEM((FFN_SLOTS, bm * ROW_CHUNKS, LANES), F32), pltpu.VMEM((bm, D_MODEL), BF16),
                        pltpu.SemaphoreType.DMA((FFN_SLOTS,))],
    )
    return pl.pallas_call(
        functools.partial(_ffn_kernel, bm=bm),
        grid_spec=grid_spec,
        out_shape=jax.ShapeDtypeStruct((nb * bm * ROW_CHUNKS, LANES), F32),
        compiler_params=_cparams(("arbitrary",)),
        name="expert_ffn",
    )(block_expert, block_valid, idx3, idx3, idx3, xn, w1, b1, w2, b2)


def _combine_kernel(idx0_ref, idxn_ref, h_ref, route_ref, fn_ref, ys_hbm, o_ref, ybuf, sem, *, tm, nt):
    i = pl.program_id(0)
    slot = i % 2
    rows = TOP_K * tm

    @pl.when(i == 0)
    def _():
        _gather_rows(idx0_ref, ys_hbm, ybuf.at[0], sem.at[0], 0, rows, DMA_UNROLL)

    @pl.when(i + 1 < nt)
    def _():
        _gather_rows(idxn_ref, ys_hbm, ybuf.at[1 - slot], sem.at[1 - slot], 0, rows, DMA_UNROLL)

    _wait_rows(ys_hbm, ybuf.at[slot], sem.at[slot], rows)
    acc = h_ref[...]
    for k in range(TOP_K):
        acc = acc + route_ref[:, TOP_K + k:TOP_K + k + 1] * _rows_to_lanes(ybuf, (slot,), k * tm, tm, F32)
    o_ref[...] = _rms(acc, fn_ref[...])


def _combine(dest_t, h, route, fn, ys):
    n = h.shape[0]
    tm = COMB_TM
    nt = n // tm
    rows = TOP_K * tm
    smem_blk = lambda fn_: pl.BlockSpec((1, 1, rows), fn_, memory_space=pltpu.SMEM)
    return pl.pallas_call(
        functools.partial(_combine_kernel, tm=tm, nt=nt),
        grid=(nt,),
        in_specs=[smem_blk(lambda i: (0, 0, 0)), smem_blk(lambda i: (jnp.minimum(i + 1, nt - 1), 0, 0)),
                  pl.BlockSpec((tm, D_MODEL), lambda i: (i, 0)), pl.BlockSpec((tm, LANES), lambda i: (i, 0)),
                  pl.BlockSpec((1, D_MODEL), lambda i: (0, 0)), pl.BlockSpec(memory_space=pl.ANY)],
        out_specs=pl.BlockSpec((tm, D_MODEL), lambda i: (i, 0)),
        out_shape=jax.ShapeDtypeStruct((n, D_MODEL), F32),
        scratch_shapes=[pltpu.VMEM((2, rows * ROW_CHUNKS, LANES), F32), pltpu.SemaphoreType.DMA((2,))],
        compiler_params=_cparams(("arbitrary",)),
        name="combine",
    )(dest_t, dest_t, h, route, fn, ys)


def _rope_tables(positions, rot, lead, period):
    inv_freq = ROPE_THETA ** (-jnp.arange(0, rot, 2, dtype=F32) / rot)
    ang = positions.astype(F32)[..., None] * inv_freq
    cos, sin = jnp.cos(ang), jnp.sin(ang)
    shape = cos.shape[:-1]
    tail = period - lead - rot
    c = jnp.concatenate([jnp.ones(shape + (lead,), F32), cos, cos, jnp.ones(shape + (tail,), F32)], axis=-1)
    s = jnp.concatenate([jnp.zeros(shape + (lead,), F32), -sin, sin, jnp.zeros(shape + (tail,), F32)], axis=-1)
    reps = LANES // period
    return jnp.tile(c, reps), jnp.tile(s, reps)


def _layout_w_in(w):
    d = w.shape[0]
    sizes = (MLA_Q_RANK, MLA_KV_RANK, MLA_ROPE_DIM, NSA_HEADS * NSA_HEAD_DIM) + (NSA_KV_HEADS * NSA_HEAD_DIM,) * 6 \
        + (3 * NSA_HEADS,)
    offs = [0]
    for sz in sizes:
        offs.append(offs[-1] + sz)
    seg = [w[:, offs[j]:offs[j + 1]] for j in range(len(sizes))]
    z = lambda n: jnp.zeros((d, n), w.dtype)
    kpe = jnp.concatenate([z(MLA_NOPE_DIM), seg[2], z(LANES - MLA_NOPE_DIM - MLA_ROPE_DIM)], axis=1)
    gates = jnp.concatenate([seg[10], z(LANES - 3 * NSA_HEADS)], axis=1)
    out = jnp.concatenate([seg[0], seg[1], kpe, seg[3]] + seg[4:10] + [gates], axis=1)
    assert out.shape[1] == C_TOTAL
    return out


def _layout_mla_up(w_q_up, w_kv_up):
    rq = w_q_up.shape[0]
    qd = MLA_NOPE_DIM + MLA_ROPE_DIM
    wq = w_q_up.reshape(rq, MLA_HEADS, qd)
    wq = jnp.pad(wq, ((0, 0), (0, 0), (0, LANES - qd))).reshape(rq, MLA_HEADS * LANES)
    rk = w_kv_up.shape[0]
    wkv = w_kv_up.reshape(rk, MLA_HEADS, MLA_NOPE_DIM + MLA_V_DIM)
    wk = jnp.pad(wkv[:, :, :MLA_NOPE_DIM], ((0, 0), (0, 0), (0, LANES - MLA_NOPE_DIM))).reshape(rk, MLA_HEADS * LANES)
    wv = wkv[:, :, MLA_NOPE_DIM:].reshape(rk, MLA_HEADS * MLA_V_DIM)
    return wq, wk, wv


def _layout_compress(cmp_pos, cmp_w1, cmp_w2):
    g = NSA_KV_HEADS
    eye = jnp.eye(g, dtype=F32)
    half = CMP_BLOCK // 2
    assert CMP_BLOCK == 2 * CMP_STRIDE

    def w1_part(w):
        return jnp.einsum('ldh,gk->lgdkh', w, eye).reshape(half * g * NSA_HEAD_DIM, g * CMP_HIDDEN)

    def pos_part(p):
        return jnp.broadcast_to(p[:, None, :], (half, g, NSA_HEAD_DIM)).reshape(1, -1)

    w1a = jnp.stack([w1_part(cmp_w1[i, :half]) for i in range(2)]).astype(BF16)
    w1b = jnp.stack([w1_part(cmp_w1[i, half:]) for i in range(2)]).astype(BF16)
    w2 = jnp.stack([jnp.einsum('hd,gk->ghkd', cmp_w2[i], eye).reshape(g * CMP_HIDDEN, g * NSA_HEAD_DIM)
                    for i in range(2)]).astype(BF16)
    pos = jnp.stack([jnp.concatenate([pos_part(cmp_pos[i, :half]), pos_part(cmp_pos[i, half:])], axis=0)
                     for i in range(2)])
    return pos, w1a, w1b, w2


def _overlap_matrix(rows, n_cmp):
    n = jnp.arange(rows)[:, None]
    j = jnp.arange(LANES)[None, :] % HALF
    start = n * CMP_STRIDE
    ov = (start < (j + 1) * SEL_BLOCK) & (start + CMP_BLOCK > j * SEL_BLOCK) & (n < n_cmp)
    return ov.astype(BF16)


def _gate_expand_matrix():
    width = NSA_HEADS * NSA_HEAD_DIM
    rows = jnp.arange(LANES)[:, None]
    cols = jnp.arange(3 * width)[None, :]
    branch, head = cols // width, (cols % width) // NSA_HEAD_DIM
    return (rows == head * 3 + branch).astype(BF16)


def kernel(x, positions, attn_norm, w_in, mla_q_norm, mla_w_q_up, mla_kv_norm, mla_w_kv_up, nsa_cmp_pos,
           nsa_cmp_w1, nsa_cmp_w2, w_out, ffn_norm, router_w, router_b, moe_w1, moe_b1, moe_w2, moe_b2, final_norm):
    b, s, d = x.shape
    n = b * s
    assert attn_norm.shape[0] == 1, "single-layer configuration"
    assert d == D_MODEL and s % SEL_BLOCK == 0 and s // SEL_BLOCK <= HALF and s % ATT_T == 0
    assert s % PROJ_TM == 0 and ATT_T % PROJ_TM == 0 and n % POST_TM == 0 and n % COMB_TM == 0

    mla_c, mla_s = _rope_tables(positions.reshape(n), MLA_ROPE_DIM, MLA_NOPE_DIM, LANES)
    nsa_c, nsa_s = _rope_tables(positions.reshape(n), NSA_ROPE_DIM, 0, NSA_HEAD_DIM)
    rows = s // CMP_STRIDE
    n_cmp = (s - CMP_BLOCK) // CMP_STRIDE + 1
    cmp_end = jnp.minimum(jnp.arange(rows) * CMP_STRIDE + CMP_BLOCK - 1, s - 1)
    cmp_c, cmp_s = _rope_tables(positions[:, cmp_end], NSA_ROPE_DIM, 0, NSA_HEAD_DIM)
    ov = _overlap_matrix(rows, n_cmp)
    eg = _gate_expand_matrix()

    x2 = x.reshape(n, d)
    win = _layout_w_in(w_in[0]).astype(BF16)
    wq, wk, wv = (w.astype(BF16) for w in _layout_mla_up(mla_w_q_up[0], mla_w_kv_up[0]))
    (qm, km, vmt, qs, kc_in, vc_in, ksx, vst, kwx, vwt, gl) = _proj(
        x2, (mla_c, mla_s, nsa_c, nsa_s), attn_norm[0][None], win, mla_q_norm[0][None], wq,
        mla_kv_norm[0][None], wk, wv, b, s)

    pos, w1a, w1b, w2c = _layout_compress(nsa_cmp_pos[0], nsa_cmp_w1[0], nsa_cmp_w2[0])
    flat = CMP_STRIDE * NSA_KV_HEADS * NSA_HEAD_DIM
    kcx, vcx = _compress(kc_in.reshape(b, rows, flat), vc_in.reshape(b, rows, flat), pos, w1a, w1b, w2c,
                         cmp_c, cmp_s)

    o_mla = _mla_attention(qm.reshape(b, s, -1), km.reshape(b, s, -1), vmt)
    o_cmp, qa = _cmp_attention(qs.reshape(b, s, -1), kcx, vcx, ov)
    o_sel, o_win = _selwin_attention(qa, ksx.reshape(b, s, -1), vst, kwx.reshape(b, s, -1), vwt)

    rw = jnp.pad(router_w[0], ((0, 0), (0, LANES - N_EXPERTS)))
    rw_hi = rw.astype(BF16)
    rw_lo = (rw - rw_hi.astype(F32)).astype(BF16)
    rb = jnp.pad(router_b[0], (0, LANES - N_EXPERTS))[None]
    hres, xn, route, counts = _post(
        x2, o_mla.reshape(n, -1), o_cmp.reshape(n, -1), o_sel.reshape(n, -1), o_win.reshape(n, -1), gl, eg,
        w_out[0].astype(BF16), ffn_norm[0][None], rw_hi, rw_lo, rb)

    top_idx = route[:, 0:TOP_K].astype(I32)
    rank = route[:, 2 * TOP_K:3 * TOP_K].astype(I32)
    cnt = counts[0, :N_EXPERTS].astype(I32)
    padded = ((cnt + FFN_BM - 1) // FFN_BM) * FFN_BM
    pend = jnp.cumsum(padded)
    pstart = pend - padded
    dest = pstart[top_idx] + rank
    nb = -(-(n * TOP_K) // FFN_BM) + N_EXPERTS + FFN_AHEAD
    block_row = jnp.arange(nb, dtype=I32) * FFN_BM
    block_expert = jnp.minimum(jnp.sum((block_row[:, None] >= pend[None, :]).astype(I32), axis=1), N_EXPERTS - 1)
    block_valid = (block_row < pend[-1]).astype(I32)
    tok_ids = jnp.broadcast_to(jnp.arange(n, dtype=I32)[:, None], (n, TOP_K))
    grouped = jnp.sort((top_idx * n + tok_ids).reshape(-1)) % n
    grouped = jnp.concatenate([grouped, jnp.zeros((FFN_BM,), I32)])
    cstart = jnp.cumsum(cnt) - cnt
    block_src = jnp.where(block_valid > 0, cstart[block_expert] + block_row - pstart[block_expert], 0)
    row_tok = jax.vmap(lambda st: lax.dynamic_slice(grouped, (st,), (FFN_BM,)))(block_src).reshape(-1)

    ys = _ffn(block_expert, block_valid, row_tok, xn, moe_w1[0].astype(BF16), moe_b1[0][:, None, :],
              moe_w2[0].astype(BF16), moe_b2[0][:, None, :])
    dest_t = dest.reshape(n // COMB_TM, COMB_TM, TOP_K).transpose(0, 2, 1).reshape(n // COMB_TM, 1, TOP_K * COMB_TM)
    out = _combine(dest_t, hres, route, final_norm[None], ys)
    return out.reshape(b, s, d)
```

```python
import functools
import math

import jax
import jax.numpy as jnp
from jax import lax
from jax.experimental import pallas as pl
from jax.experimental.pallas import tpu as pltpu

F32 = jnp.float32
BF16 = jnp.bfloat16
I32 = jnp.int32

D_MODEL = 1024
ROPE_THETA = 500000.0
NORM_EPS = 1e-5
NEG_INF = -1e30
POS_INF = 1e30
LOG2E = math.log2(math.e)

MLA_HEADS = 8
MLA_NOPE_DIM = 64
MLA_ROPE_DIM = 32
MLA_V_DIM = 64
MLA_Q_RANK = 256
MLA_KV_RANK = 128

NSA_HEADS = 8
NSA_KV_HEADS = 2
NSA_GROUP = NSA_HEADS // NSA_KV_HEADS
NSA_HEAD_DIM = 64
NSA_ROPE_DIM = NSA_HEAD_DIM // 4
CMP_BLOCK = 32
CMP_STRIDE = 16
CMP_HIDDEN = 2 * NSA_HEAD_DIM
SEL_BLOCK = 64
N_SEL = 16
N_LOCAL_SEL = 2
WINDOW = 512

N_EXPERTS = 32
TOP_K = 4
D_FF = D_MODEL
SWIGLU_LIMIT = 7.0
SWIGLU_ALPHA = 1.702

LANES = 128
HALF = LANES // 2
VMEM_LIMIT = 48 * 1024 * 1024

PROJ_TM = 256
ATT_T = 512
ATT_QS = 256
ATT_KC_FULL = 512
ATT_KC_DIAG = 256
ATT_AHEAD = 4
MLA_TILES = 4
SEL_TILES = 2
V_PAD = 16
CMP_TQ = 256
POST_TM = 256
FFN_BM = 512
COMB_TM = 128
FFN_AHEAD = 2
FFN_SLOTS = FFN_AHEAD + 1
DMA_UNROLL = 8
ROW_CHUNKS = D_MODEL // LANES

C_QLAT = 0
C_KVLAT = C_QLAT + MLA_Q_RANK
C_KPE = C_KVLAT + MLA_KV_RANK
C_QNSA = C_KPE + LANES
C_KC = C_QNSA + NSA_HEADS * NSA_HEAD_DIM
C_VC = C_KC + LANES
C_KS = C_VC + LANES
C_VS = C_KS + LANES
C_KW = C_VS + LANES
C_VW = C_KW + LANES
C_GATE = C_VW + LANES
C_TOTAL = C_GATE + LANES


def _cparams(sem):
    return pltpu.CompilerParams(dimension_semantics=sem, vmem_limit_bytes=VMEM_LIMIT)


def _rms(x, g):
    return x * lax.rsqrt(jnp.mean(x * x, axis=-1, keepdims=True) + NORM_EPS) * g


def _sigmoid(x):
    return 1.0 / (1.0 + jnp.exp(-x))


def _rope(x, c, s, half, first):
    n = x.shape[-1]
    partner = jnp.where(first, pltpu.roll(x, n - half, 1), pltpu.roll(x, half, 1))
    return x * c + partner * s


def _expand_pair(x, lo, fill):
    y = pltpu.roll(x, HALF, 1)
    return (jnp.where(lo, x, fill), jnp.where(lo, fill, y), jnp.where(lo, y, fill), jnp.where(lo, fill, x))


def _proj_kernel(x_ref, mc_ref, ms_ref, nc_ref, ns_ref, an_ref, win_ref, qn_ref, wq_ref, kvn_ref, wk_ref,
                 wv_ref, qm_ref, km_ref, vm_ref, qs_ref, kc_ref, vc_ref, ks_ref, vs_ref, kw_ref, vw_ref,
                 g_ref, *, tm, seq):
    x = x_ref[...]
    u = _rms(x, an_ref[...])
    y = jnp.dot(u.astype(BF16), win_ref[...], preferred_element_type=F32)
    lane = lax.broadcasted_iota(I32, (1, LANES), 1)
    lo = lane < HALF
    mla_first = lane < MLA_NOPE_DIM + MLA_ROPE_DIM // 2
    nsa_first = (lane % NSA_HEAD_DIM) < NSA_ROPE_DIM // 2
    mc, ms, nc, ns = mc_ref[...], ms_ref[...], nc_ref[...], ns_ref[...]

    qn = _rms(y[:, C_QLAT:C_QLAT + MLA_Q_RANK], qn_ref[...])
    q = jnp.dot(qn.astype(BF16), wq_ref[...], preferred_element_type=F32)
    q = q * ((MLA_NOPE_DIM + MLA_ROPE_DIM) ** -0.5 * LOG2E)
    for h in range(MLA_HEADS):
        sl = slice(h * LANES, (h + 1) * LANES)
        qm_ref[:, sl] = _rope(q[:, sl], mc, ms, MLA_ROPE_DIM // 2, mla_first).astype(BF16)
    kvn = _rms(y[:, C_KVLAT:C_KVLAT + MLA_KV_RANK], kvn_ref[...]).astype(BF16)
    kpe = _rope(y[:, C_KPE:C_KPE + LANES], mc, ms, MLA_ROPE_DIM // 2, mla_first)
    kn = jnp.dot(kvn, wk_ref[...], preferred_element_type=F32)
    for h in range(MLA_HEADS):
        sl = slice(h * LANES, (h + 1) * LANES)
        km_ref[:, sl] = (kn[:, sl] + kpe).astype(BF16)
    ones_pad = (lax.broadcasted_iota(I32, (V_PAD, tm), 0) == 0).astype(BF16)
    vmt = jnp.dot(kvn, wv_ref[...], preferred_element_type=F32).T.astype(BF16)
    for h in range(MLA_HEADS):
        base = h * (MLA_V_DIM + V_PAD)
        vm_ref[0, 0, base:base + MLA_V_DIM, :] = vmt[h * MLA_V_DIM:(h + 1) * MLA_V_DIM]
        vm_ref[0, 0, base + MLA_V_DIM:base + MLA_V_DIM + V_PAD, :] = ones_pad

    for c in range(NSA_HEADS * NSA_HEAD_DIM // LANES):
        ch = y[:, C_QNSA + c * LANES:C_QNSA + (c + 1) * LANES]
        qs_ref[:, c * LANES:(c + 1) * LANES] = (
            _rope(ch, nc, ns, NSA_ROPE_DIM // 2, nsa_first) * (NSA_HEAD_DIM ** -0.5 * LOG2E)).astype(BF16)
    kc_ref[...] = y[:, C_KC:C_KC + LANES]
    vc_ref[...] = y[:, C_VC:C_VC + LANES]
    row = lax.broadcasted_iota(I32, (tm, LANES), 0)
    tok = (pl.program_id(0) * tm + row) % seq
    onehot = ((lane % HALF) == tok // SEL_BLOCK).astype(F32)
    ks = _rope(y[:, C_KS:C_KS + LANES], nc, ns, NSA_ROPE_DIM // 2, nsa_first)
    kw = _rope(y[:, C_KW:C_KW + LANES], nc, ns, NSA_ROPE_DIM // 2, nsa_first)
    for ref, val, fill in ((ks_ref, ks, onehot), (kw_ref, kw, 0.0)):
        for c, chunk in enumerate(_expand_pair(val, lo, fill)):
            ref[:, c * LANES:(c + 1) * LANES] = chunk.astype(BF16)
    for ref, col in ((vs_ref, C_VS), (vw_ref, C_VW)):
        vt = y[:, col:col + LANES].T.astype(BF16)
        for g in range(NSA_KV_HEADS):
            base = g * (NSA_HEAD_DIM + V_PAD)
            ref[0, 0, base:base + NSA_HEAD_DIM, :] = vt[g * NSA_HEAD_DIM:(g + 1) * NSA_HEAD_DIM]
            ref[0, 0, base + NSA_HEAD_DIM:base + NSA_HEAD_DIM + V_PAD, :] = ones_pad
    g_ref[...] = y[:, C_GATE:C_GATE + LANES]


def _proj(x2, tabs, an, win, qn, wq, kvn, wk, wv, batch, seq):
    n = x2.shape[0]
    tm = PROJ_TM
    per_b, per_slab = seq // tm, ATT_T // tm
    tok = lambda w: pl.BlockSpec((tm, w), lambda i: (i, 0))
    tr = lambda r: pl.BlockSpec((1, 1, r, tm),
                                lambda i: (i // per_b, (i % per_b) // per_slab, 0, (i % per_b) % per_slab))
    full = lambda a: pl.BlockSpec(a.shape, lambda i: (0,) * a.ndim)
    tok_out = lambda w, d: (tok(w), jax.ShapeDtypeStruct((n, w), d))
    tr_out = lambda r: (tr(r), jax.ShapeDtypeStruct((batch, seq // ATT_T, r, ATT_T), BF16))
    outs = [tok_out(8 * LANES, BF16), tok_out(8 * LANES, BF16), tr_out(MLA_HEADS * (MLA_V_DIM + V_PAD)),
            tok_out(4 * LANES, BF16), tok_out(LANES, F32), tok_out(LANES, F32),
            tok_out(4 * LANES, BF16), tr_out(NSA_KV_HEADS * (NSA_HEAD_DIM + V_PAD)), tok_out(4 * LANES, BF16),
            tr_out(NSA_KV_HEADS * (NSA_HEAD_DIM + V_PAD)), tok_out(LANES, F32)]
    return pl.pallas_call(
        functools.partial(_proj_kernel, tm=tm, seq=seq),
        grid=(n // tm,),
        in_specs=[tok(D_MODEL)] + [tok(LANES)] * 4 + [full(a) for a in (an, win, qn, wq, kvn, wk, wv)],
        out_specs=[o[0] for o in outs],
        out_shape=[o[1] for o in outs],
        compiler_params=_cparams(("parallel",)),
        name="proj",
    )(x2, *tabs, an, win, qn, wq, kvn, wk, wv)


def _compress_kernel(kin_ref, vin_ref, pos_ref, w1a_ref, w1b_ref, w2_ref, cc_ref, cs_ref, kc_ref, vc_ref, *, rows):
    lane = lax.broadcasted_iota(I32, (1, LANES), 1)
    lo = lane < HALF
    nsa_first = (lane % NSA_HEAD_DIM) < NSA_ROPE_DIM // 2

    def comp(x, i):
        p = jnp.dot((x + pos_ref[i, 0:1, :]).astype(BF16), w1a_ref[i], preferred_element_type=F32)
        q = jnp.dot((x + pos_ref[i, 1:2, :]).astype(BF16), w1b_ref[i], preferred_element_type=F32)
        hid = p + pltpu.roll(q, rows - 1, 0)
        hid = hid * _sigmoid(hid)
        return jnp.dot(hid.astype(BF16), w2_ref[i], preferred_element_type=F32)

    kc = _rope(comp(kin_ref[0], 0), cc_ref[0], cs_ref[0], NSA_ROPE_DIM // 2, nsa_first)
    vc = comp(vin_ref[0], 1)
    for ref, val in ((kc_ref, kc), (vc_ref, vc)):
        for c, chunk in enumerate(_expand_pair(val, lo, 0.0)):
            ref[0, :, c * LANES:(c + 1) * LANES] = chunk.astype(BF16)


def _compress(kin, vin, pos, w1a, w1b, w2, cc, cs):
    b, rows, width = kin.shape
    per_b = lambda w: pl.BlockSpec((1, rows, w), lambda i: (i, 0, 0))
    full = lambda a: pl.BlockSpec(a.shape, lambda i: (0,) * a.ndim)
    return pl.pallas_call(
        functools.partial(_compress_kernel, rows=rows),
        grid=(b,),
        in_specs=[per_b(width), per_b(width), full(pos), full(w1a), full(w1b), full(w2), per_b(LANES), per_b(LANES)],
        out_specs=[per_b(4 * LANES), per_b(4 * LANES)],
        out_shape=[jax.ShapeDtypeStruct((b, rows, 4 * LANES), BF16)] * 2,
        compiler_params=_cparams(("parallel",)),
        name="compress",
    )(kin, vin, pos, w1a, w1b, w2, cc, cs)


def _subtile_plan(t, kc, delta, window):
    plan = []
    for j in range(t // ATT_QS):
        for c in range(t // kc):
            if delta is None:
                plan.append((j, c, False))
                continue
            dmin = delta + j * ATT_QS - (c * kc + kc - 1)
            dmax = delta + j * ATT_QS + ATT_QS - 1 - c * kc
            if dmax < 0 or (window is not None and dmin >= window):
                continue
            full = dmin >= 0 and (window is None or dmax < window)
            plan.append((j, c, not full))
    return plan


def _flash_tiles(q_of, heads, t, tiles):
    jsl = lambda j: slice(j * ATT_QS, (j + 1) * ATT_QS)
    items = []
    for tile in tiles:
        plan = _subtile_plan(t, tile["kc"], tile["delta"], tile["window"])
        for h in heads:
            for j in sorted({jj for jj, _, _ in plan}):
                items += [(tile, h, j, c, masked) for jj, c, masked in plan if jj == j]
    keys = []
    for tile, h, j, _, _ in items:
        key = (id(tile["m_ref"]), h, j)
        if key not in [k for k, _ in keys]:
            keys.append((key, (tile["m_ref"], tile["acc_ref"], h, j)))
    state = {key: (m_ref[h:h + 1, jsl(j)], acc_ref[h, :, jsl(j)]) for key, (m_ref, acc_ref, h, j) in keys}

    def scores(item):
        tile, h, j, c, masked = item
        kc = tile["kc"]
        st = lax.dot_general(tile["k_of"](h, c, kc), q_of(h, j), (((1,), (1,)), ((), ())),
                             preferred_element_type=F32)
        if masked:
            kk = lax.broadcasted_iota(I32, (kc, ATT_QS), 0)
            qq = lax.broadcasted_iota(I32, (kc, ATT_QS), 1)
            d = (tile["delta"] + j * ATT_QS - c * kc) + qq - kk
            ok = d >= 0
            if tile["window"] is not None:
                ok = ok & (d < tile["window"])
            st = jnp.where(ok, st, NEG_INF)
        return st

    def fold(pending):
        key, a_prev, pv_prev = pending
        mm, aa = state[key]
        state[key] = (mm, aa * a_prev + pv_prev)

    ahead = [scores(it) for it in items[:ATT_AHEAD]]
    pending = None
    for n, (tile, h, j, c, _) in enumerate(items):
        st = ahead.pop(0)
        if n + ATT_AHEAD < len(items):
            ahead.append(scores(items[n + ATT_AHEAD]))
        key = (id(tile["m_ref"]), h, j)
        m = state[key][0]
        m_new = jnp.maximum(m, jnp.max(st, axis=0, keepdims=True))
        alpha = jnp.exp2(m - m_new)
        p = jnp.exp2((st - m_new).astype(BF16))
        pv = jnp.dot(tile["vt_of"](h, c, tile["kc"]), p, preferred_element_type=F32)
        if pending is not None:
            fold(pending)
        state[key] = (m_new, state[key][1])
        pending = (key, alpha, pv)
    fold(pending)
    for key, (m_ref, acc_ref, h, j) in keys:
        m, acc = state[key]
        m_ref[h:h + 1, jsl(j)] = m
        acc_ref[h, :, jsl(j)] = acc


def _flash_init(m_ref, acc_ref):
    m_ref[...] = jnp.full(m_ref.shape, NEG_INF, F32)
    acc_ref[...] = jnp.zeros(acc_ref.shape, F32)


def _flash_out_pair(acc_ref, pair, dv):
    parts = [acc_ref[h, 0:dv, :] / acc_ref[h, dv:dv + 1, :] for h in (2 * pair, 2 * pair + 1)]
    return jnp.concatenate(parts, axis=0).T


def _kv_readers(k_ref, vt_ref, kt, t, k_chunk_of_head, v_rows_of_head):
    def k_of(h, c, kc):
        start = pl.multiple_of(kt * t + c * kc, kc)
        return k_ref[0, pl.ds(start, kc), k_chunk_of_head(h)]

    def vt_of(h, c, kc):
        return vt_ref[0, kt, v_rows_of_head(h), c * kc:(c + 1) * kc]

    return k_of, vt_of


def _mla_kernel(q_ref, k_ref, vt_ref, o_ref, m_ref, acc_ref, *, t):
    qi = pl.program_id(2)
    heads = (0, 1)
    chunk = lambda h: slice(h * LANES, (h + 1) * LANES)
    vrows = lambda h: slice(h * (MLA_V_DIM + V_PAD), (h + 1) * (MLA_V_DIM + V_PAD))
    q_of = lambda h, j: q_ref[0, j * ATT_QS:(j + 1) * ATT_QS, chunk(h)]
    _flash_init(m_ref, acc_ref)

    def tile(kt, kc, delta):
        k_of, vt_of = _kv_readers(k_ref, vt_ref, kt, t, chunk, vrows)
        return dict(k_of=k_of, vt_of=vt_of, kc=kc, delta=delta, window=None, m_ref=m_ref, acc_ref=acc_ref)

    def body(kp, carry):
        _flash_tiles(q_of, heads, t, [tile(MLA_TILES * kp + u, ATT_KC_FULL, None) for u in range(MLA_TILES)])
        return carry

    lax.fori_loop(0, qi // MLA_TILES, body, 0)
    for rem in range(MLA_TILES):
        @pl.when(qi % MLA_TILES == rem)
        def _(rem=rem):
            _flash_tiles(q_of, heads, t, [tile(qi - rem + u, ATT_KC_FULL, None) for u in range(rem)]
                         + [tile(qi, ATT_KC_DIAG, 0)])

    o_ref[0] = _flash_out_pair(acc_ref, 0, MLA_V_DIM)


def _mla_attention(q, k, vt):
    b, s, _ = q.shape
    t = ATT_T
    pairs = MLA_HEADS // 2
    return pl.pallas_call(
        functools.partial(_mla_kernel, t=t),
        grid=(b, pairs, s // t),
        in_specs=[pl.BlockSpec((1, t, 2 * LANES), lambda bi, p, qi: (bi, qi, p)),
                  pl.BlockSpec((1, s, 2 * LANES), lambda bi, p, qi: (bi, 0, p)),
                  pl.BlockSpec((1, s // t, 2 * (MLA_V_DIM + V_PAD), t), lambda bi, p, qi: (bi, 0, p, 0))],
        out_specs=pl.BlockSpec((1, t, LANES), lambda bi, p, qi: (bi, qi, p)),
        out_shape=jax.ShapeDtypeStruct((b, s, pairs * LANES), F32),
        scratch_shapes=[pltpu.VMEM((2, t), F32), pltpu.VMEM((2, MLA_V_DIM + V_PAD, t), F32)],
        compiler_params=_cparams(("parallel", "parallel", "arbitrary")),
        name="mla_attention",
    )(q, k, vt)


def _cmp_kernel(q_ref, kc_ref, vc_ref, ov_ref, o_ref, qa_ref, *, tq, rows):
    qi = pl.program_id(2)
    lane = lax.broadcasted_iota(I32, (1, LANES), 1)
    lo = lane < HALF
    t_pos = qi * tq + lax.broadcasted_iota(I32, (tq, 1), 0)
    cmp_end = lax.broadcasted_iota(I32, (1, rows), 1) * CMP_STRIDE + (CMP_BLOCK - 1)
    mask = cmp_end <= t_pos
    live = (t_pos >= CMP_BLOCK - 1).astype(F32)

    psum = jnp.zeros((tq, rows), F32)
    for pair in range(NSA_GROUP // 2):
        qp = q_ref[0, :, pair * LANES:(pair + 1) * LANES]
        acc = jnp.zeros((tq, LANES), F32)
        for e in range(2):
            sl = slice(e * LANES, (e + 1) * LANES)
            s = lax.dot_general(qp, kc_ref[0, :, sl], (((1,), (1,)), ((), ())), preferred_element_type=F32)
            s = jnp.where(mask, s, NEG_INF)
            p = jnp.exp2(s - jnp.max(s, axis=-1, keepdims=True))
            p = p / jnp.sum(p, axis=-1, keepdims=True) * live
            psum = psum + p
            acc = acc + jnp.dot(p.astype(BF16), vc_ref[0, :, sl], preferred_element_type=F32)
        o_ref[0, :, pair * LANES:(pair + 1) * LANES] = acc

    p_hi = psum.astype(BF16)
    p_lo = (psum - p_hi.astype(F32)).astype(BF16)
    imp = (jnp.dot(p_hi, ov_ref[...], preferred_element_type=F32)
           + jnp.dot(p_lo, ov_ref[...], preferred_element_type=F32))
    blk = lane % HALF
    cur = t_pos // SEL_BLOCK
    forced = (blk == 0) | ((blk <= cur) & (blk > cur - N_LOCAL_SEL))
    val = jnp.where(forced, POS_INF, jnp.where(blk <= cur, imp, NEG_INF))

    vt = val.T[:HALF]
    sub = 8
    groups = [vt[a * sub:(a + 1) * sub] for a in range(HALF // sub)]
    cnts = [jnp.zeros((sub, tq), I32) for _ in groups]
    jsub = lax.broadcasted_iota(I32, (sub, tq), 0)
    for i in range(HALF):
        r = vt[i:i + 1, :]
        for a, grp in enumerate(groups):
            if a > i // sub:
                ahead = r >= grp
            elif a < i // sub:
                ahead = r > grp
            else:
                ahead = (r > grp) | ((r == grp) & (i % sub < jsub))
            cnts[a] = cnts[a] + ahead.astype(I32)
    cnt = jnp.concatenate(cnts, axis=0)
    bias_t = jnp.where(cnt < N_SEL, 0.0, NEG_INF).astype(F32)
    bias = jnp.concatenate([bias_t, bias_t], axis=0).T.astype(BF16)
    for h in range(NSA_GROUP):
        qp = q_ref[0, :, (h // 2) * LANES:(h // 2 + 1) * LANES]
        chunk = jnp.where(lo, qp, bias) if h % 2 == 0 else jnp.where(lo, bias, qp)
        qa_ref[0, :, h * LANES:(h + 1) * LANES] = chunk


def _cmp_attention(q, kcx, vcx, ov):
    b, s, _ = q.shape
    rows = kcx.shape[1]
    tq = min(CMP_TQ, s)
    g = NSA_KV_HEADS
    return pl.pallas_call(
        functools.partial(_cmp_kernel, tq=tq, rows=rows),
        grid=(b, g, s // tq),
        in_specs=[pl.BlockSpec((1, tq, 2 * LANES), lambda bi, gi, qi: (bi, qi, gi)),
                  pl.BlockSpec((1, rows, 2 * LANES), lambda bi, gi, qi: (bi, 0, gi)),
                  pl.BlockSpec((1, rows, 2 * LANES), lambda bi, gi, qi: (bi, 0, gi)),
                  pl.BlockSpec(ov.shape, lambda bi, gi, qi: (0, 0))],
        out_specs=[pl.BlockSpec((1, tq, 2 * LANES), lambda bi, gi, qi: (bi, qi, gi)),
                   pl.BlockSpec((1, tq, 4 * LANES), lambda bi, gi, qi: (bi, qi, gi))],
        out_shape=[jax.ShapeDtypeStruct((b, s, g * 2 * LANES), F32),
                   jax.ShapeDtypeStruct((b, s, g * 4 * LANES), BF16)],
        compiler_params=_cparams(("parallel", "parallel", "parallel")),
        name="cmp_attention",
    )(q, kcx, vcx, ov)


def _selwin_kernel(qa_ref, ks_ref, vs_ref, kw_ref, vw_ref, os_ref, ow_ref, ms_ref, as_ref, mw_ref, aw_ref, *, t):
    qi = pl.program_id(2)
    heads = tuple(range(NSA_GROUP))
    chunk = lambda h: slice((h % 2) * LANES, (h % 2 + 1) * LANES)
    vrows = lambda h: slice(0, NSA_HEAD_DIM + V_PAD)
    q_of = lambda h, j: qa_ref[0, j * ATT_QS:(j + 1) * ATT_QS, h * LANES:(h + 1) * LANES]
    _flash_init(ms_ref, as_ref)
    _flash_init(mw_ref, aw_ref)

    def sel_tile(kt, kc, delta):
        k_of, vt_of = _kv_readers(ks_ref, vs_ref, kt, t, chunk, vrows)
        return dict(k_of=k_of, vt_of=vt_of, kc=kc, delta=delta, window=None, m_ref=ms_ref, acc_ref=as_ref)

    def win_tile(kt, delta):
        k_of, vt_of = _kv_readers(kw_ref, vw_ref, kt, t, chunk, vrows)
        return dict(k_of=k_of, vt_of=vt_of, kc=ATT_KC_DIAG, delta=delta, window=WINDOW, m_ref=mw_ref, acc_ref=aw_ref)

    def body(kp, carry):
        _flash_tiles(q_of, heads, t, [sel_tile(SEL_TILES * kp + u, ATT_KC_FULL, None) for u in range(SEL_TILES)])
        return carry

    lax.fori_loop(0, qi // SEL_TILES, body, 0)

    for rem in range(SEL_TILES):
        @pl.when((qi % SEL_TILES == rem) & (qi > 0))
        def _(rem=rem):
            _flash_tiles(q_of, heads, t, [sel_tile(qi - rem + u, ATT_KC_FULL, None) for u in range(rem)]
                         + [win_tile(qi - 1, t), sel_tile(qi, ATT_KC_DIAG, 0), win_tile(qi, 0)])

    @pl.when(qi == 0)
    def _():
        _flash_tiles(q_of, heads, t, [sel_tile(qi, ATT_KC_DIAG, 0), win_tile(qi, 0)])

    for pair in range(NSA_GROUP // 2):
        sl = slice(pair * LANES, (pair + 1) * LANES)
        os_ref[0, :, sl] = _flash_out_pair(as_ref, pair, NSA_HEAD_DIM)
        ow_ref[0, :, sl] = _flash_out_pair(aw_ref, pair, NSA_HEAD_DIM)


def _selwin_attention(qa, ksx, vst, kwx, vwt):
    b, s, _ = qa.shape
    t = ATT_T
    assert t >= WINDOW, "window branch reads only the previous and the diagonal key tile"
    g = NSA_KV_HEADS
    pairs = NSA_GROUP // 2
    kspec = pl.BlockSpec((1, s, 2 * LANES), lambda bi, gi, qi: (bi, 0, gi))
    vspec = pl.BlockSpec((1, s // t, NSA_HEAD_DIM + V_PAD, t), lambda bi, gi, qi: (bi, 0, gi, 0))
    ospec = pl.BlockSpec((1, t, pairs * LANES), lambda bi, gi, qi: (bi, qi, gi))
    stat = lambda: pltpu.VMEM((NSA_GROUP, t), F32)
    accs = lambda: pltpu.VMEM((NSA_GROUP, NSA_HEAD_DIM + V_PAD, t), F32)
    return pl.pallas_call(
        functools.partial(_selwin_kernel, t=t),
        grid=(b, g, s // t),
        in_specs=[pl.BlockSpec((1, t, NSA_GROUP * LANES), lambda bi, gi, qi: (bi, qi, gi)),
                  kspec, vspec, kspec, vspec],
        out_specs=[ospec, ospec],
        out_shape=[jax.ShapeDtypeStruct((b, s, g * pairs * LANES), F32)] * 2,
        scratch_shapes=[stat(), accs(), stat(), accs()],
        compiler_params=_cparams(("parallel", "parallel", "arbitrary")),
        name="selwin_attention",
    )(qa, ksx, vst, kwx, vwt)


def _post_kernel(x_ref, om_ref, oc_ref, os_ref, ow_ref, g_ref, eg_ref, wo_ref, fn_ref, rwh_ref, rwl_ref, rb_ref,
                 h_ref, xn_ref, route_ref, cnt_ref, *, tm):
    i = pl.program_id(0)
    width = NSA_HEADS * NSA_HEAD_DIM
    sg = _sigmoid(g_ref[...])
    sg_hi = sg.astype(BF16)
    sg_lo = (sg - sg_hi.astype(F32)).astype(BF16)
    ge = (jnp.dot(sg_hi, eg_ref[...], preferred_element_type=F32)
          + jnp.dot(sg_lo, eg_ref[...], preferred_element_type=F32))
    o_nsa = (ge[:, 0:width] * oc_ref[...] + ge[:, width:2 * width] * os_ref[...]
             + ge[:, 2 * width:3 * width] * ow_ref[...])
    mla_w = MLA_HEADS * MLA_V_DIM
    mixed = (jnp.dot(om_ref[...].astype(BF16), wo_ref[0:mla_w, :], preferred_element_type=F32)
             + jnp.dot(o_nsa.astype(BF16), wo_ref[mla_w:mla_w + width, :], preferred_element_type=F32))
    h = x_ref[...] + mixed
    h_ref[...] = h
    xn = _rms(h, fn_ref[...])
    _lanes_to_rows(xn_ref, xn)

    x_hi = xn.astype(BF16)
    x_lo = (xn - x_hi.astype(F32)).astype(BF16)
    logits = (jnp.dot(x_hi, rwh_ref[...], preferred_element_type=F32)
              + jnp.dot(x_hi, rwl_ref[...], preferred_element_type=F32)
              + jnp.dot(x_lo, rwh_ref[...], preferred_element_type=F32)) + rb_ref[...]
    lane = lax.broadcasted_iota(I32, (tm, LANES), 1)
    lg = jnp.where(lane < N_EXPERTS, logits, -jnp.inf)
    vals, hots = [], []
    for _ in range(TOP_K):
        m = jnp.max(lg, axis=-1, keepdims=True)
        idx = jnp.min(jnp.where(lg == m, lane, LANES), axis=-1, keepdims=True)
        hot = lane == idx
        lg = jnp.where(hot, -jnp.inf, lg)
        vals.append(m)
        hots.append(hot)
    es = [jnp.exp(v - vals[0]) for v in vals]
    den = es[0] + es[1] + es[2] + es[3]

    @pl.when(i == 0)
    def _():
        cnt_ref[...] = jnp.zeros(cnt_ref.shape, F32)

    hot_all = (hots[0] | hots[1] | hots[2] | hots[3]).astype(F32)
    r = lax.broadcasted_iota(I32, (tm, tm), 0)
    c = lax.broadcasted_iota(I32, (tm, tm), 1)
    tri = (c < r).astype(BF16)
    before = jnp.dot(tri, hot_all.astype(BF16), preferred_element_type=F32) + cnt_ref[...]
    route = jnp.zeros((tm, LANES), F32)
    for k in range(TOP_K):
        e_k = jnp.sum(jnp.where(hots[k], lane, 0), axis=-1, keepdims=True).astype(F32)
        rank_k = jnp.sum(jnp.where(hots[k], before, 0.0), axis=-1, keepdims=True)
        route = (route + jnp.where(lane == k, e_k, 0.0) + jnp.where(lane == TOP_K + k, es[k] / den, 0.0)
                 + jnp.where(lane == 2 * TOP_K + k, rank_k, 0.0))
    route_ref[...] = route
    cnt_ref[...] = cnt_ref[...] + jnp.sum(hot_all, axis=0, keepdims=True)


def _post(x2, om, oc, os_, ow, gl, eg, wo, fn, rwh, rwl, rb):
    n = x2.shape[0]
    tm = POST_TM
    tok = lambda w: pl.BlockSpec((tm, w), lambda i: (i, 0))
    full = lambda a: pl.BlockSpec(a.shape, lambda i: (0,) * a.ndim)
    return pl.pallas_call(
        functools.partial(_post_kernel, tm=tm),
        grid=(n // tm,),
        in_specs=[tok(D_MODEL), tok(om.shape[1]), tok(oc.shape[1]), tok(os_.shape[1]), tok(ow.shape[1]), tok(LANES)]
        + [full(a) for a in (eg, wo, fn, rwh, rwl, rb)],
        out_specs=[tok(D_MODEL), pl.BlockSpec((tm * ROW_CHUNKS, LANES), lambda i: (i, 0)), tok(LANES),
                   pl.BlockSpec((1, LANES), lambda i: (0, 0))],
        out_shape=[jax.ShapeDtypeStruct((n, D_MODEL), F32), jax.ShapeDtypeStruct((n * ROW_CHUNKS, LANES), F32),
                   jax.ShapeDtypeStruct((n, LANES), F32), jax.ShapeDtypeStruct((1, LANES), F32)],
        compiler_params=_cparams(("arbitrary",)),
        name="post",
    )(x2, om, oc, os_, ow, gl, eg, wo, fn, rwh, rwl, rb)


def _gather_rows(idx_ref, src_hbm, dst, sem, first, count, unroll):
    def body(r, carry):
        src = pl.multiple_of(idx_ref[0, 0, first + r] * ROW_CHUNKS, ROW_CHUNKS)
        dst_row = pl.multiple_of((first + r) * ROW_CHUNKS, ROW_CHUNKS)
        pltpu.make_async_copy(src_hbm.at[pl.ds(src, ROW_CHUNKS), :], dst.at[pl.ds(dst_row, ROW_CHUNKS), :], sem).start()
        return carry
    lax.fori_loop(0, count, body, 0, unroll=unroll)


def _wait_rows(src_hbm, dst, sem, count):
    pltpu.make_async_copy(src_hbm.at[pl.ds(0, count * ROW_CHUNKS), :], dst, sem).wait()


def _rows_to_lanes(ref, lead, first, rows, dtype):
    return jnp.concatenate(
        [ref[lead + (pl.ds(first * ROW_CHUNKS + c, rows, stride=ROW_CHUNKS), slice(None))].astype(dtype)
         for c in range(ROW_CHUNKS)], axis=1)


def _lanes_to_rows(ref, val):
    for c in range(ROW_CHUNKS):
        ref[pl.ds(c, val.shape[0], stride=ROW_CHUNKS), :] = val[:, c * LANES:(c + 1) * LANES]


def _ffn_kernel(be_ref, bv_ref, idx0_ref, idx1_ref, idxn_ref, xn_hbm, w1_ref, b1_ref, w2_ref, b2_ref, y_ref, xbuf, xb_ref,
                sem, *, bm):
    del be_ref
    i = pl.program_id(0)
    slot = i % FFN_SLOTS

    @pl.when(i == 0)
    def _():
        _gather_rows(idx0_ref, xn_hbm, xbuf.at[0], sem.at[0], 0, bm, DMA_UNROLL)
        _gather_rows(idx1_ref, xn_hbm, xbuf.at[1], sem.at[1], 0, bm, DMA_UNROLL)

    @pl.when(bv_ref[i] > 0)
    def _():
        _wait_rows(xn_hbm, xbuf.at[slot], sem.at[slot], bm)
        xb_ref[...] = _rows_to_lanes(xbuf, (slot,), 0, bm, BF16)
        ahead = (i + FFN_AHEAD) % FFN_SLOTS
        _gather_rows(idxn_ref, xn_hbm, xbuf.at[ahead], sem.at[ahead], 0, bm, True)
        hcat = jnp.dot(xb_ref[...], w1_ref[0], preferred_element_type=F32) + b1_ref[0]
        a = jnp.minimum(hcat[:, :D_FF], SWIGLU_LIMIT)
        up = jnp.clip(hcat[:, D_FF:], -SWIGLU_LIMIT, SWIGLU_LIMIT)
        glu = a * _sigmoid(SWIGLU_ALPHA * a)
        y = jnp.dot(((up + 1.0) * glu).astype(BF16), w2_ref[0], preferred_element_type=F32) + b2_ref[0]
        _lanes_to_rows(y_ref, y)

    @pl.when(bv_ref[i] == 0)
    def _():
        @pl.when(bv_ref[jnp.maximum(i - FFN_AHEAD, 0)] > 0)
        def _():
            _wait_rows(xn_hbm, xbuf.at[slot], sem.at[slot], bm)

        y_ref[...] = jnp.zeros(y_ref.shape, F32)


def _ffn(block_expert, block_valid, row_tok, xn, w1, b1, w2, b2):
    nb = block_expert.shape[0]
    bm = FFN_BM
    idx3 = row_tok.reshape(nb, 1, bm)
    smem_blk = lambda fn: pl.BlockSpec((1, 1, bm), fn, memory_space=pltpu.SMEM)
    grid_spec = pltpu.PrefetchScalarGridSpec(
        num_scalar_prefetch=2,
        grid=(nb,),
        in_specs=[smem_blk(lambda i, be, bv: (0, 0, 0)),
                  smem_blk(lambda i, be, bv: (1, 0, 0)),
                  smem_blk(lambda i, be, bv: (jnp.minimum(i + FFN_AHEAD, nb - 1), 0, 0)),
                  pl.BlockSpec(memory_space=pl.ANY),
                  pl.BlockSpec((1, D_MODEL, 2 * D_FF), lambda i, be, bv: (be[i], 0, 0)),
                  pl.BlockSpec((1, 1, 2 * D_FF), lambda i, be, bv: (be[i], 0, 0)),
                  pl.BlockSpec((1, D_FF, D_MODEL), lambda i, be, bv: (be[i], 0, 0)),
                  pl.BlockSpec((1, 1, D_MODEL), lambda i, be, bv: (be[i], 0, 0))],
        out_specs=pl.BlockSpec((bm * ROW_CHUNKS, LANES), lambda i, be, bv: (i, 0)),
        scratch_shapes=[pltpu.VMEM((FFN_SLOTS, bm * ROW_CHUNKS, LANES), F32), pltpu.VMEM((bm, D_MODEL), BF16),
                        pltpu.SemaphoreType.DMA((FFN_SLOTS,))],
    )
    return pl.pallas_call(
        functools.partial(_ffn_kernel, bm=bm),
        grid_spec=grid_spec,
        out_shape=jax.ShapeDtypeStruct((nb * bm * ROW_CHUNKS, LANES), F32),
        compiler_params=_cparams(("arbitrary",)),
        name="expert_ffn",
    )(block_expert, block_valid, idx3, idx3, idx3, xn, w1, b1, w2, b2)


def _combine_kernel(idx0_ref, idxn_ref, h_ref, route_ref, fn_ref, ys_hbm, o_ref, ybuf, sem, *, tm, nt):
    i = pl.program_id(0)
    slot = i % 2
    rows = TOP_K * tm

    @pl.when(i == 0)
    def _():
        _gather_rows(idx0_ref, ys_hbm, ybuf.at[0], sem.at[0], 0, rows, DMA_UNROLL)

    @pl.when(i + 1 < nt)
    def _():
        _gather_rows(idxn_ref, ys_hbm, ybuf.at[1 - slot], sem.at[1 - slot], 0, rows, DMA_UNROLL)

    _wait_rows(ys_hbm, ybuf.at[slot], sem.at[slot], rows)
    acc = h_ref[...]
    for k in range(TOP_K):
        acc = acc + route_ref[:, TOP_K + k:TOP_K + k + 1] * _rows_to_lanes(ybuf, (slot,), k * tm, tm, F32)
    o_ref[...] = _rms(acc, fn_ref[...])


def _combine(dest_t, h, route, fn, ys):
    n = h.shape[0]
    tm = COMB_TM
    nt = n // tm
    rows = TOP_K * tm
    smem_blk = lambda fn_: pl.BlockSpec((1, 1, rows), fn_, memory_space=pltpu.SMEM)
    return pl.pallas_call(
        functools.partial(_combine_kernel, tm=tm, nt=nt),
        grid=(nt,),
        in_specs=[smem_blk(lambda i: (0, 0, 0)), smem_blk(lambda i: (jnp.minimum(i + 1, nt - 1), 0, 0)),
                  pl.BlockSpec((tm, D_MODEL), lambda i: (i, 0)), pl.BlockSpec((tm, LANES), lambda i: (i, 0)),
                  pl.BlockSpec((1, D_MODEL), lambda i: (0, 0)), pl.BlockSpec(memory_space=pl.ANY)],
        out_specs=pl.BlockSpec((tm, D_MODEL), lambda i: (i, 0)),
        out_shape=jax.ShapeDtypeStruct((n, D_MODEL), F32),
        scratch_shapes=[pltpu.VMEM((2, rows * ROW_CHUNKS, LANES), F32), pltpu.SemaphoreType.DMA((2,))],
        compiler_params=_cparams(("arbitrary",)),
        name="combine",
    )(dest_t, dest_t, h, route, fn, ys)


def _rope_tables(positions, rot, lead, period):
    inv_freq = ROPE_THETA ** (-jnp.arange(0, rot, 2, dtype=F32) / rot)
    ang = positions.astype(F32)[..., None] * inv_freq
    cos, sin = jnp.cos(ang), jnp.sin(ang)
    shape = cos.shape[:-1]
    tail = period - lead - rot
    c = jnp.concatenate([jnp.ones(shape + (lead,), F32), cos, cos, jnp.ones(shape + (tail,), F32)], axis=-1)
    s = jnp.concatenate([jnp.zeros(shape + (lead,), F32), -sin, sin, jnp.zeros(shape + (tail,), F32)], axis=-1)
    reps = LANES // period
    return jnp.tile(c, reps), jnp.tile(s, reps)


def _layout_w_in(w):
    d = w.shape[0]
    sizes = (MLA_Q_RANK, MLA_KV_RANK, MLA_ROPE_DIM, NSA_HEADS * NSA_HEAD_DIM) + (NSA_KV_HEADS * NSA_HEAD_DIM,) * 6 \
        + (3 * NSA_HEADS,)
    offs = [0]
    for sz in sizes:
        offs.append(offs[-1] + sz)
    seg = [w[:, offs[j]:offs[j + 1]] for j in range(len(sizes))]
    z = lambda n: jnp.zeros((d, n), w.dtype)
    kpe = jnp.concatenate([z(MLA_NOPE_DIM), seg[2], z(LANES - MLA_NOPE_DIM - MLA_ROPE_DIM)], axis=1)
    gates = jnp.concatenate([seg[10], z(LANES - 3 * NSA_HEADS)], axis=1)
    out = jnp.concatenate([seg[0], seg[1], kpe, seg[3]] + seg[4:10] + [gates], axis=1)
    assert out.shape[1] == C_TOTAL
    return out


def _layout_mla_up(w_q_up, w_kv_up):
    rq = w_q_up.shape[0]
    qd = MLA_NOPE_DIM + MLA_ROPE_DIM
    wq = w_q_up.reshape(rq, MLA_HEADS, qd)
    wq = jnp.pad(wq, ((0, 0), (0, 0), (0, LANES - qd))).reshape(rq, MLA_HEADS * LANES)
    rk = w_kv_up.shape[0]
    wkv = w_kv_up.reshape(rk, MLA_HEADS, MLA_NOPE_DIM + MLA_V_DIM)
    wk = jnp.pad(wkv[:, :, :MLA_NOPE_DIM], ((0, 0), (0, 0), (0, LANES - MLA_NOPE_DIM))).reshape(rk, MLA_HEADS * LANES)
    wv = wkv[:, :, MLA_NOPE_DIM:].reshape(rk, MLA_HEADS * MLA_V_DIM)
    return wq, wk, wv


def _layout_compress(cmp_pos, cmp_w1, cmp_w2):
    g = NSA_KV_HEADS
    eye = jnp.eye(g, dtype=F32)
    half = CMP_BLOCK // 2
    assert CMP_BLOCK == 2 * CMP_STRIDE

    def w1_part(w):
        return jnp.einsum('ldh,gk->lgdkh', w, eye).reshape(half * g * NSA_HEAD_DIM, g * CMP_HIDDEN)

    def pos_part(p):
        return jnp.broadcast_to(p[:, None, :], (half, g, NSA_HEAD_DIM)).reshape(1, -1)

    w1a = jnp.stack([w1_part(cmp_w1[i, :half]) for i in range(2)]).astype(BF16)
    w1b = jnp.stack([w1_part(cmp_w1[i, half:]) for i in range(2)]).astype(BF16)
    w2 = jnp.stack([jnp.einsum('hd,gk->ghkd', cmp_w2[i], eye).reshape(g * CMP_HIDDEN, g * NSA_HEAD_DIM)
                    for i in range(2)]).astype(BF16)
    pos = jnp.stack([jnp.concatenate([pos_part(cmp_pos[i, :half]), pos_part(cmp_pos[i, half:])], axis=0)
                     for i in range(2)])
    return pos, w1a, w1b, w2


def _overlap_matrix(rows, n_cmp):
    n = jnp.arange(rows)[:, None]
    j = jnp.arange(LANES)[None, :] % HALF
    start = n * CMP_STRIDE
    ov = (start < (j + 1) * SEL_BLOCK) & (start + CMP_BLOCK > j * SEL_BLOCK) & (n < n_cmp)
    return ov.astype(BF16)


def _gate_expand_matrix():
    width = NSA_HEADS * NSA_HEAD_DIM
    rows = jnp.arange(LANES)[:, None]
    cols = jnp.arange(3 * width)[None, :]
    branch, head = cols // width, (cols % width) // NSA_HEAD_DIM
    return (rows == head * 3 + branch).astype(BF16)


def kernel(x, positions, attn_norm, w_in, mla_q_norm, mla_w_q_up, mla_kv_norm, mla_w_kv_up, nsa_cmp_pos,
           nsa_cmp_w1, nsa_cmp_w2, w_out, ffn_norm, router_w, router_b, moe_w1, moe_b1, moe_w2, moe_b2, final_norm):
    b, s, d = x.shape
    n = b * s
    assert attn_norm.shape[0] == 1, "single-layer configuration"
    assert d == D_MODEL and s % SEL_BLOCK == 0 and s // SEL_BLOCK <= HALF and s % ATT_T == 0
    assert s % PROJ_TM == 0 and ATT_T % PROJ_TM == 0 and n % POST_TM == 0 and n % COMB_TM == 0

    mla_c, mla_s = _rope_tables(positions.reshape(n), MLA_ROPE_DIM, MLA_NOPE_DIM, LANES)
    nsa_c, nsa_s = _rope_tables(positions.reshape(n), NSA_ROPE_DIM, 0, NSA_HEAD_DIM)
    rows = s // CMP_STRIDE
    n_cmp = (s - CMP_BLOCK) // CMP_STRIDE + 1
    cmp_end = jnp.minimum(jnp.arange(rows) * CMP_STRIDE + CMP_BLOCK - 1, s - 1)
    cmp_c, cmp_s = _rope_tables(positions[:, cmp_end], NSA_ROPE_DIM, 0, NSA_HEAD_DIM)
    ov = _overlap_matrix(rows, n_cmp)
    eg = _gate_expand_matrix()

    x2 = x.reshape(n, d)
    win = _layout_w_in(w_in[0]).astype(BF16)
    wq, wk, wv = (w.astype(BF16) for w in _layout_mla_up(mla_w_q_up[0], mla_w_kv_up[0]))
    (qm, km, vmt, qs, kc_in, vc_in, ksx, vst, kwx, vwt, gl) = _proj(
        x2, (mla_c, mla_s, nsa_c, nsa_s), attn_norm[0][None], win, mla_q_norm[0][None], wq,
        mla_kv_norm[0][None], wk, wv, b, s)

    pos, w1a, w1b, w2c = _layout_compress(nsa_cmp_pos[0], nsa_cmp_w1[0], nsa_cmp_w2[0])
    flat = CMP_STRIDE * NSA_KV_HEADS * NSA_HEAD_DIM
    kcx, vcx = _compress(kc_in.reshape(b, rows, flat), vc_in.reshape(b, rows, flat), pos, w1a, w1b, w2c,
                         cmp_c, cmp_s)

    o_mla = _mla_attention(qm.reshape(b, s, -1), km.reshape(b, s, -1), vmt)
    o_cmp, qa = _cmp_attention(qs.reshape(b, s, -1), kcx, vcx, ov)
    o_sel, o_win = _selwin_attention(qa, ksx.reshape(b, s, -1), vst, kwx.reshape(b, s, -1), vwt)

    rw = jnp.pad(router_w[0], ((0, 0), (0, LANES - N_EXPERTS)))
    rw_hi = rw.astype(BF16)
    rw_lo = (rw - rw_hi.astype(F32)).astype(BF16)
    rb = jnp.pad(router_b[0], (0, LANES - N_EXPERTS))[None]
    hres, xn, route, counts = _post(
        x2, o_mla.reshape(n, -1), o_cmp.reshape(n, -1), o_sel.reshape(n, -1), o_win.reshape(n, -1), gl, eg,
        w_out[0].astype(BF16), ffn_norm[0][None], rw_hi, rw_lo, rb)

    top_idx = route[:, 0:TOP_K].astype(I32)
    rank = route[:, 2 * TOP_K:3 * TOP_K].astype(I32)
    cnt = counts[0, :N_EXPERTS].astype(I32)
    padded = ((cnt + FFN_BM - 1) // FFN_BM) * FFN_BM
    pend = jnp.cumsum(padded)
    pstart = pend - padded
    dest = pstart[top_idx] + rank
    nb = -(-(n * TOP_K) // FFN_BM) + N_EXPERTS + FFN_AHEAD
    block_row = jnp.arange(nb, dtype=I32) * FFN_BM
    block_expert = jnp.minimum(jnp.sum((block_row[:, None] >= pend[None, :]).astype(I32), axis=1), N_EXPERTS - 1)
    block_valid = (block_row < pend[-1]).astype(I32)
    tok_ids = jnp.broadcast_to(jnp.arange(n, dtype=I32)[:, None], (n, TOP_K))
    real = (top_idx * (2 * n) + tok_ids).reshape(-1)
    fill_end = jnp.cumsum(padded - cnt)
    fill_id = jnp.arange(nb * FFN_BM - n * TOP_K, dtype=I32)
    fill_expert = jnp.sum((fill_id[:, None] >= fill_end[None, :]).astype(I32), axis=1)
    low = jnp.sort(jnp.concatenate([real, fill_expert * (2 * n) + n])) % (2 * n)
    row_tok = jnp.where(low < n, low, 0)

    ys = _ffn(block_expert, block_valid, row_tok, xn, moe_w1[0].astype(BF16), moe_b1[0][:, None, :],
              moe_w2[0].astype(BF16), moe_b2[0][:, None, :])
    dest_t = dest.reshape(n // COMB_TM, COMB_TM, TOP_K).transpose(0, 2, 1).reshape(n // COMB_TM, 1, TOP_K * COMB_TM)
    out = _combine(dest_t, hres, route, final_norm[None], ys)
    return out.reshape(b, s, d)
```

```python
import functools
import math

import jax
import jax.numpy as jnp
from jax import lax
from jax.experimental import pallas as pl
from jax.experimental.pallas import tpu as pltpu

F32 = jnp.float32
BF16 = jnp.bfloat16
I32 = jnp.int32

D_MODEL = 1024
ROPE_THETA = 500000.0
NORM_EPS = 1e-5
NEG_INF = -1e30
POS_INF = 1e30
LOG2E = math.log2(math.e)

MLA_HEADS = 8
MLA_NOPE_DIM = 64
MLA_ROPE_DIM = 32
MLA_V_DIM = 64
MLA_Q_RANK = 256
MLA_KV_RANK = 128

NSA_HEADS = 8
NSA_KV_HEADS = 2
NSA_GROUP = NSA_HEADS // NSA_KV_HEADS
NSA_HEAD_DIM = 64
NSA_ROPE_DIM = NSA_HEAD_DIM // 4
CMP_BLOCK = 32
CMP_STRIDE = 16
CMP_HIDDEN = 2 * NSA_HEAD_DIM
SEL_BLOCK = 64
N_SEL = 16
N_LOCAL_SEL = 2
WINDOW = 512

N_EXPERTS = 32
TOP_K = 4
D_FF = D_MODEL
SWIGLU_LIMIT = 7.0
SWIGLU_ALPHA = 1.702

LANES = 128
HALF = LANES // 2
VMEM_LIMIT = 48 * 1024 * 1024

PROJ_TM = 256
ATT_T = 512
ATT_QS = 256
ATT_KC_FULL = 512
ATT_KC_DIAG = 256
ATT_AHEAD = 4
MLA_TILES = 4
SEL_TILES = 2
V_PAD = 16
CMP_TQ = 256
POST_TM = 256
FFN_BM = 512
COMB_TM = 128
FFN_AHEAD = 2
FFN_SLOTS = FFN_AHEAD + 1
DMA_UNROLL = 8
ROW_CHUNKS = D_MODEL // LANES

C_QLAT = 0
C_KVLAT = C_QLAT + MLA_Q_RANK
C_KPE = C_KVLAT + MLA_KV_RANK
C_QNSA = C_KPE + LANES
C_KC = C_QNSA + NSA_HEADS * NSA_HEAD_DIM
C_VC = C_KC + LANES
C_KS = C_VC + LANES
C_VS = C_KS + LANES
C_KW = C_VS + LANES
C_VW = C_KW + LANES
C_GATE = C_VW + LANES
C_TOTAL = C_GATE + LANES


def _cparams(sem):
    return pltpu.CompilerParams(dimension_semantics=sem, vmem_limit_bytes=VMEM_LIMIT)


def _rms(x, g):
    return x * lax.rsqrt(jnp.mean(x * x, axis=-1, keepdims=True) + NORM_EPS) * g


def _sigmoid(x):
    return 1.0 / (1.0 + jnp.exp(-x))


def _rope(x, c, s, half, first):
    n = x.shape[-1]
    partner = jnp.where(first, pltpu.roll(x, n - half, 1), pltpu.roll(x, half, 1))
    return x * c + partner * s


def _expand_pair(x, lo, fill):
    y = pltpu.roll(x, HALF, 1)
    return (jnp.where(lo, x, fill), jnp.where(lo, fill, y), jnp.where(lo, y, fill), jnp.where(lo, fill, x))


def _proj_kernel(x_ref, mc_ref, ms_ref, nc_ref, ns_ref, an_ref, win_ref, qn_ref, wq_ref, kvn_ref, wk_ref,
                 wv_ref, qm_ref, km_ref, vm_ref, qs_ref, kc_ref, vc_ref, ks_ref, vs_ref, kw_ref, vw_ref,
                 g_ref, *, tm, seq):
    x = x_ref[...]
    u = _rms(x, an_ref[...])
    y = jnp.dot(u.astype(BF16), win_ref[...], preferred_element_type=F32)
    lane = lax.broadcasted_iota(I32, (1, LANES), 1)
    lo = lane < HALF
    mla_first = lane < MLA_NOPE_DIM + MLA_ROPE_DIM // 2
    nsa_first = (lane % NSA_HEAD_DIM) < NSA_ROPE_DIM // 2
    mc, ms, nc, ns = mc_ref[...], ms_ref[...], nc_ref[...], ns_ref[...]

    qn = _rms(y[:, C_QLAT:C_QLAT + MLA_Q_RANK], qn_ref[...])
    q = jnp.dot(qn.astype(BF16), wq_ref[...], preferred_element_type=F32)
    q = q * ((MLA_NOPE_DIM + MLA_ROPE_DIM) ** -0.5 * LOG2E)
    for h in range(MLA_HEADS):
        sl = slice(h * LANES, (h + 1) * LANES)
        qm_ref[:, sl] = _rope(q[:, sl], mc, ms, MLA_ROPE_DIM // 2, mla_first).astype(BF16)
    kvn = _rms(y[:, C_KVLAT:C_KVLAT + MLA_KV_RANK], kvn_ref[...]).astype(BF16)
    kpe = _rope(y[:, C_KPE:C_KPE + LANES], mc, ms, MLA_ROPE_DIM // 2, mla_first)
    kn = jnp.dot(kvn, wk_ref[...], preferred_element_type=F32)
    for h in range(MLA_HEADS):
        sl = slice(h * LANES, (h + 1) * LANES)
        km_ref[:, sl] = (kn[:, sl] + kpe).astype(BF16)
    ones_pad = (lax.broadcasted_iota(I32, (V_PAD, tm), 0) == 0).astype(BF16)
    vmt = jnp.dot(kvn, wv_ref[...], preferred_element_type=F32).T.astype(BF16)
    for h in range(MLA_HEADS):
        base = h * (MLA_V_DIM + V_PAD)
        vm_ref[0, 0, base:base + MLA_V_DIM, :] = vmt[h * MLA_V_DIM:(h + 1) * MLA_V_DIM]
        vm_ref[0, 0, base + MLA_V_DIM:base + MLA_V_DIM + V_PAD, :] = ones_pad

    for c in range(NSA_HEADS * NSA_HEAD_DIM // LANES):
        ch = y[:, C_QNSA + c * LANES:C_QNSA + (c + 1) * LANES]
        qs_ref[:, c * LANES:(c + 1) * LANES] = (
            _rope(ch, nc, ns, NSA_ROPE_DIM // 2, nsa_first) * (NSA_HEAD_DIM ** -0.5 * LOG2E)).astype(BF16)
    kc_ref[...] = y[:, C_KC:C_KC + LANES]
    vc_ref[...] = y[:, C_VC:C_VC + LANES]
    row = lax.broadcasted_iota(I32, (tm, LANES), 0)
    tok = (pl.program_id(0) * tm + row) % seq
    onehot = ((lane % HALF) == tok // SEL_BLOCK).astype(F32)
    ks = _rope(y[:, C_KS:C_KS + LANES], nc, ns, NSA_ROPE_DIM // 2, nsa_first)
    kw = _rope(y[:, C_KW:C_KW + LANES], nc, ns, NSA_ROPE_DIM // 2, nsa_first)
    for ref, val, fill in ((ks_ref, ks, onehot), (kw_ref, kw, 0.0)):
        for c, chunk in enumerate(_expand_pair(val, lo, fill)):
            ref[:, c * LANES:(c + 1) * LANES] = chunk.astype(BF16)
    for ref, col in ((vs_ref, C_VS), (vw_ref, C_VW)):
        vt = y[:, col:col + LANES].T.astype(BF16)
        for g in range(NSA_KV_HEADS):
            base = g * (NSA_HEAD_DIM + V_PAD)
            ref[0, 0, base:base + NSA_HEAD_DIM, :] = vt[g * NSA_HEAD_DIM:(g + 1) * NSA_HEAD_DIM]
            ref[0, 0, base + NSA_HEAD_DIM:base + NSA_HEAD_DIM + V_PAD, :] = ones_pad
    g_ref[...] = y[:, C_GATE:C_GATE + LANES]


def _proj(x2, tabs, an, win, qn, wq, kvn, wk, wv, batch, seq):
    n = x2.shape[0]
    tm = PROJ_TM
    per_b, per_slab = seq // tm, ATT_T // tm
    tok = lambda w: pl.BlockSpec((tm, w), lambda i: (i, 0))
    tr = lambda r: pl.BlockSpec((1, 1, r, tm),
                                lambda i: (i // per_b, (i % per_b) // per_slab, 0, (i % per_b) % per_slab))
    full = lambda a: pl.BlockSpec(a.shape, lambda i: (0,) * a.ndim)
    tok_out = lambda w, d: (tok(w), jax.ShapeDtypeStruct((n, w), d))
    tr_out = lambda r: (tr(r), jax.ShapeDtypeStruct((batch, seq // ATT_T, r, ATT_T), BF16))
    outs = [tok_out(8 * LANES, BF16), tok_out(8 * LANES, BF16), tr_out(MLA_HEADS * (MLA_V_DIM + V_PAD)),
            tok_out(4 * LANES, BF16), tok_out(LANES, F32), tok_out(LANES, F32),
            tok_out(4 * LANES, BF16), tr_out(NSA_KV_HEADS * (NSA_HEAD_DIM + V_PAD)), tok_out(4 * LANES, BF16),
            tr_out(NSA_KV_HEADS * (NSA_HEAD_DIM + V_PAD)), tok_out(LANES, F32)]
    return pl.pallas_call(
        functools.partial(_proj_kernel, tm=tm, seq=seq),
        grid=(n // tm,),
        in_specs=[tok(D_MODEL)] + [tok(LANES)] * 4 + [full(a) for a in (an, win, qn, wq, kvn, wk, wv)],
        out_specs=[o[0] for o in outs],
        out_shape=[o[1] for o in outs],
        compiler_params=_cparams(("parallel",)),
        name="proj",
    )(x2, *tabs, an, win, qn, wq, kvn, wk, wv)


def _compress_kernel(kin_ref, vin_ref, pos_ref, w1a_ref, w1b_ref, w2_ref, cc_ref, cs_ref, kc_ref, vc_ref, *, rows):
    lane = lax.broadcasted_iota(I32, (1, LANES), 1)
    lo = lane < HALF
    nsa_first = (lane % NSA_HEAD_DIM) < NSA_ROPE_DIM // 2

    def comp(x, i):
        p = jnp.dot((x + pos_ref[i, 0:1, :]).astype(BF16), w1a_ref[i], preferred_element_type=F32)
        q = jnp.dot((x + pos_ref[i, 1:2, :]).astype(BF16), w1b_ref[i], preferred_element_type=F32)
        hid = p + pltpu.roll(q, rows - 1, 0)
        hid = hid * _sigmoid(hid)
        return jnp.dot(hid.astype(BF16), w2_ref[i], preferred_element_type=F32)

    kc = _rope(comp(kin_ref[0], 0), cc_ref[0], cs_ref[0], NSA_ROPE_DIM // 2, nsa_first)
    vc = comp(vin_ref[0], 1)
    for ref, val in ((kc_ref, kc), (vc_ref, vc)):
        for c, chunk in enumerate(_expand_pair(val, lo, 0.0)):
            ref[0, :, c * LANES:(c + 1) * LANES] = chunk.astype(BF16)


def _compress(kin, vin, pos, w1a, w1b, w2, cc, cs):
    b, rows, width = kin.shape
    per_b = lambda w: pl.BlockSpec((1, rows, w), lambda i: (i, 0, 0))
    full = lambda a: pl.BlockSpec(a.shape, lambda i: (0,) * a.ndim)
    return pl.pallas_call(
        functools.partial(_compress_kernel, rows=rows),
        grid=(b,),
        in_specs=[per_b(width), per_b(width), full(pos), full(w1a), full(w1b), full(w2), per_b(LANES), per_b(LANES)],
        out_specs=[per_b(4 * LANES), per_b(4 * LANES)],
        out_shape=[jax.ShapeDtypeStruct((b, rows, 4 * LANES), BF16)] * 2,
        compiler_params=_cparams(("parallel",)),
        name="compress",
    )(kin, vin, pos, w1a, w1b, w2, cc, cs)


def _subtile_plan(t, kc, delta, window):
    plan = []
    for j in range(t // ATT_QS):
        for c in range(t // kc):
            if delta is None:
                plan.append((j, c, False))
                continue
            dmin = delta + j * ATT_QS - (c * kc + kc - 1)
            dmax = delta + j * ATT_QS + ATT_QS - 1 - c * kc
            if dmax < 0 or (window is not None and dmin >= window):
                continue
            full = dmin >= 0 and (window is None or dmax < window)
            plan.append((j, c, not full))
    return plan


def _flash_tiles(q_of, heads, t, tiles):
    jsl = lambda j: slice(j * ATT_QS, (j + 1) * ATT_QS)
    items = []
    for tile in tiles:
        plan = _subtile_plan(t, tile["kc"], tile["delta"], tile["window"])
        for h in heads:
            for j in sorted({jj for jj, _, _ in plan}):
                items += [(tile, h, j, c, masked) for jj, c, masked in plan if jj == j]
    keys = []
    for tile, h, j, _, _ in items:
        key = (id(tile["m_ref"]), h, j)
        if key not in [k for k, _ in keys]:
            keys.append((key, (tile["m_ref"], tile["acc_ref"], h, j)))
    state = {key: (m_ref[h:h + 1, jsl(j)], acc_ref[h, :, jsl(j)]) for key, (m_ref, acc_ref, h, j) in keys}

    def scores(item):
        tile, h, j, c, masked = item
        kc = tile["kc"]
        st = lax.dot_general(tile["k_of"](h, c, kc), q_of(h, j), (((1,), (1,)), ((), ())),
                             preferred_element_type=F32)
        if masked:
            kk = lax.broadcasted_iota(I32, (kc, ATT_QS), 0)
            qq = lax.broadcasted_iota(I32, (kc, ATT_QS), 1)
            d = (tile["delta"] + j * ATT_QS - c * kc) + qq - kk
            ok = d >= 0
            if tile["window"] is not None:
                ok = ok & (d < tile["window"])
            st = jnp.where(ok, st, NEG_INF)
        return st

    def fold(pending):
        key, a_prev, pv_prev = pending
        mm, aa = state[key]
        state[key] = (mm, aa * a_prev + pv_prev)

    ahead = [scores(it) for it in items[:ATT_AHEAD]]
    pending = None
    for n, (tile, h, j, c, _) in enumerate(items):
        st = ahead.pop(0)
        if n + ATT_AHEAD < len(items):
            ahead.append(scores(items[n + ATT_AHEAD]))
        key = (id(tile["m_ref"]), h, j)
        m = state[key][0]
        m_new = jnp.maximum(m, jnp.max(st, axis=0, keepdims=True))
        alpha = jnp.exp2(m - m_new)
        p = jnp.exp2((st - m_new).astype(BF16))
        pv = jnp.dot(tile["vt_of"](h, c, tile["kc"]), p, preferred_element_type=F32)
        if pending is not None:
            fold(pending)
        state[key] = (m_new, state[key][1])
        pending = (key, alpha, pv)
    fold(pending)
    for key, (m_ref, acc_ref, h, j) in keys:
        m, acc = state[key]
        m_ref[h:h + 1, jsl(j)] = m
        acc_ref[h, :, jsl(j)] = acc


def _flash_init(m_ref, acc_ref):
    m_ref[...] = jnp.full(m_ref.shape, NEG_INF, F32)
    acc_ref[...] = jnp.zeros(acc_ref.shape, F32)


def _flash_out_pair(acc_ref, pair, dv):
    parts = [acc_ref[h, 0:dv, :] / acc_ref[h, dv:dv + 1, :] for h in (2 * pair, 2 * pair + 1)]
    return jnp.concatenate(parts, axis=0).T


def _kv_readers(k_ref, vt_ref, kt, t, k_chunk_of_head, v_rows_of_head):
    def k_of(h, c, kc):
        start = pl.multiple_of(kt * t + c * kc, kc)
        return k_ref[0, pl.ds(start, kc), k_chunk_of_head(h)]

    def vt_of(h, c, kc):
        return vt_ref[0, kt, v_rows_of_head(h), c * kc:(c + 1) * kc]

    return k_of, vt_of


def _mla_kernel(q_ref, k_ref, vt_ref, o_ref, m_ref, acc_ref, *, t):
    qi = pl.program_id(2)
    heads = (0, 1)
    chunk = lambda h: slice(h * LANES, (h + 1) * LANES)
    vrows = lambda h: slice(h * (MLA_V_DIM + V_PAD), (h + 1) * (MLA_V_DIM + V_PAD))
    q_of = lambda h, j: q_ref[0, j * ATT_QS:(j + 1) * ATT_QS, chunk(h)]
    _flash_init(m_ref, acc_ref)

    def tile(kt, kc, delta):
        k_of, vt_of = _kv_readers(k_ref, vt_ref, kt, t, chunk, vrows)
        return dict(k_of=k_of, vt_of=vt_of, kc=kc, delta=delta, window=None, m_ref=m_ref, acc_ref=acc_ref)

    def body(kp, carry):
        _flash_tiles(q_of, heads, t, [tile(MLA_TILES * kp + u, ATT_KC_FULL, None) for u in range(MLA_TILES)])
        return carry

    lax.fori_loop(0, qi // MLA_TILES, body, 0)
    for rem in range(MLA_TILES):
        @pl.when(qi % MLA_TILES == rem)
        def _(rem=rem):
            _flash_tiles(q_of, heads, t, [tile(qi - rem + u, ATT_KC_FULL, None) for u in range(rem)]
                         + [tile(qi, ATT_KC_DIAG, 0)])

    o_ref[0] = _flash_out_pair(acc_ref, 0, MLA_V_DIM)


def _mla_attention(q, k, vt):
    b, s, _ = q.shape
    t = ATT_T
    pairs = MLA_HEADS // 2
    return pl.pallas_call(
        functools.partial(_mla_kernel, t=t),
        grid=(b, pairs, s // t),
        in_specs=[pl.BlockSpec((1, t, 2 * LANES), lambda bi, p, qi: (bi, qi, p)),
                  pl.BlockSpec((1, s, 2 * LANES), lambda bi, p, qi: (bi, 0, p)),
                  pl.BlockSpec((1, s // t, 2 * (MLA_V_DIM + V_PAD), t), lambda bi, p, qi: (bi, 0, p, 0))],
        out_specs=pl.BlockSpec((1, t, LANES), lambda bi, p, qi: (bi, qi, p)),
        out_shape=jax.ShapeDtypeStruct((b, s, pairs * LANES), F32),
        scratch_shapes=[pltpu.VMEM((2, t), F32), pltpu.VMEM((2, MLA_V_DIM + V_PAD, t), F32)],
        compiler_params=_cparams(("parallel", "parallel", "arbitrary")),
        name="mla_attention",
    )(q, k, vt)


def _cmp_kernel(q_ref, kc_ref, vc_ref, ov_ref, o_ref, qa_ref, *, tq, rows):
    qi = pl.program_id(2)
    lane = lax.broadcasted_iota(I32, (1, LANES), 1)
    lo = lane < HALF
    t_pos = qi * tq + lax.broadcasted_iota(I32, (tq, 1), 0)
    cmp_end = lax.broadcasted_iota(I32, (1, rows), 1) * CMP_STRIDE + (CMP_BLOCK - 1)
    mask = cmp_end <= t_pos
    live = (t_pos >= CMP_BLOCK - 1).astype(F32)

    psum = jnp.zeros((tq, rows), F32)
    for pair in range(NSA_GROUP // 2):
        qp = q_ref[0, :, pair * LANES:(pair + 1) * LANES]
        acc = jnp.zeros((tq, LANES), F32)
        for e in range(2):
            sl = slice(e * LANES, (e + 1) * LANES)
            s = lax.dot_general(qp, kc_ref[0, :, sl], (((1,), (1,)), ((), ())), preferred_element_type=F32)
            s = jnp.where(mask, s, NEG_INF)
            p = jnp.exp2(s - jnp.max(s, axis=-1, keepdims=True))
            p = p / jnp.sum(p, axis=-1, keepdims=True) * live
            psum = psum + p
            acc = acc + jnp.dot(p.astype(BF16), vc_ref[0, :, sl], preferred_element_type=F32)
        o_ref[0, :, pair * LANES:(pair + 1) * LANES] = acc

    p_hi = psum.astype(BF16)
    p_lo = (psum - p_hi.astype(F32)).astype(BF16)
    imp = (jnp.dot(p_hi, ov_ref[...], preferred_element_type=F32)
           + jnp.dot(p_lo, ov_ref[...], preferred_element_type=F32))
    blk = lane % HALF
    cur = t_pos // SEL_BLOCK
    forced = (blk == 0) | ((blk <= cur) & (blk > cur - N_LOCAL_SEL))
    val = jnp.where(forced, POS_INF, jnp.where(blk <= cur, imp, NEG_INF))

    vt = val.T[:HALF]
    sub = 8
    groups = [vt[a * sub:(a + 1) * sub] for a in range(HALF // sub)]
    cnts = [jnp.zeros((sub, tq), I32) for _ in groups]
    jsub = lax.broadcasted_iota(I32, (sub, tq), 0)
    for i in range(HALF):
        r = vt[i:i + 1, :]
        for a, grp in enumerate(groups):
            if a > i // sub:
                ahead = r >= grp
            elif a < i // sub:
                ahead = r > grp
            else:
                ahead = (r > grp) | ((r == grp) & (i % sub < jsub))
            cnts[a] = cnts[a] + ahead.astype(I32)
    cnt = jnp.concatenate(cnts, axis=0)
    bias_t = jnp.where(cnt < N_SEL, 0.0, NEG_INF).astype(F32)
    bias = jnp.concatenate([bias_t, bias_t], axis=0).T.astype(BF16)
    for h in range(NSA_GROUP):
        qp = q_ref[0, :, (h // 2) * LANES:(h // 2 + 1) * LANES]
        chunk = jnp.where(lo, qp, bias) if h % 2 == 0 else jnp.where(lo, bias, qp)
        qa_ref[0, :, h * LANES:(h + 1) * LANES] = chunk


def _cmp_attention(q, kcx, vcx, ov):
    b, s, _ = q.shape
    rows = kcx.shape[1]
    tq = min(CMP_TQ, s)
    g = NSA_KV_HEADS
    return pl.pallas_call(
        functools.partial(_cmp_kernel, tq=tq, rows=rows),
        grid=(b, g, s // tq),
        in_specs=[pl.BlockSpec((1, tq, 2 * LANES), lambda bi, gi, qi: (bi, qi, gi)),
                  pl.BlockSpec((1, rows, 2 * LANES), lambda bi, gi, qi: (bi, 0, gi)),
                  pl.BlockSpec((1, rows, 2 * LANES), lambda bi, gi, qi: (bi, 0, gi)),
                  pl.BlockSpec(ov.shape, lambda bi, gi, qi: (0, 0))],
        out_specs=[pl.BlockSpec((1, tq, 2 * LANES), lambda bi, gi, qi: (bi, qi, gi)),
                   pl.BlockSpec((1, tq, 4 * LANES), lambda bi, gi, qi: (bi, qi, gi))],
        out_shape=[jax.ShapeDtypeStruct((b, s, g * 2 * LANES), F32),
                   jax.ShapeDtypeStruct((b, s, g * 4 * LANES), BF16)],
        compiler_params=_cparams(("parallel", "parallel", "parallel")),
        name="cmp_attention",
    )(q, kcx, vcx, ov)


def _selwin_kernel(qa_ref, ks_ref, vs_ref, kw_ref, vw_ref, os_ref, ow_ref, ms_ref, as_ref, mw_ref, aw_ref, *, t):
    qi = pl.program_id(2)
    heads = tuple(range(NSA_GROUP))
    chunk = lambda h: slice((h % 2) * LANES, (h % 2 + 1) * LANES)
    vrows = lambda h: slice(0, NSA_HEAD_DIM + V_PAD)
    q_of = lambda h, j: qa_ref[0, j * ATT_QS:(j + 1) * ATT_QS, h * LANES:(h + 1) * LANES]
    _flash_init(ms_ref, as_ref)
    _flash_init(mw_ref, aw_ref)

    def sel_tile(kt, kc, delta):
        k_of, vt_of = _kv_readers(ks_ref, vs_ref, kt, t, chunk, vrows)
        return dict(k_of=k_of, vt_of=vt_of, kc=kc, delta=delta, window=None, m_ref=ms_ref, acc_ref=as_ref)

    def win_tile(kt, delta):
        k_of, vt_of = _kv_readers(kw_ref, vw_ref, kt, t, chunk, vrows)
        return dict(k_of=k_of, vt_of=vt_of, kc=ATT_KC_DIAG, delta=delta, window=WINDOW, m_ref=mw_ref, acc_ref=aw_ref)

    def body(kp, carry):
        _flash_tiles(q_of, heads, t, [sel_tile(SEL_TILES * kp + u, ATT_KC_FULL, None) for u in range(SEL_TILES)])
        return carry

    lax.fori_loop(0, qi // SEL_TILES, body, 0)

    for rem in range(SEL_TILES):
        @pl.when((qi % SEL_TILES == rem) & (qi > 0))
        def _(rem=rem):
            _flash_tiles(q_of, heads, t, [sel_tile(qi - rem + u, ATT_KC_FULL, None) for u in range(rem)]
                         + [win_tile(qi - 1, t), sel_tile(qi, ATT_KC_DIAG, 0), win_tile(qi, 0)])

    @pl.when(qi == 0)
    def _():
        _flash_tiles(q_of, heads, t, [sel_tile(qi, ATT_KC_DIAG, 0), win_tile(qi, 0)])

    for pair in range(NSA_GROUP // 2):
        sl = slice(pair * LANES, (pair + 1) * LANES)
        os_ref[0, :, sl] = _flash_out_pair(as_ref, pair, NSA_HEAD_DIM)
        ow_ref[0, :, sl] = _flash_out_pair(aw_ref, pair, NSA_HEAD_DIM)


def _selwin_attention(qa, ksx, vst, kwx, vwt):
    b, s, _ = qa.shape
    t = ATT_T
    assert t >= WINDOW, "window branch reads only the previous and the diagonal key tile"
    g = NSA_KV_HEADS
    pairs = NSA_GROUP // 2
    kspec = pl.BlockSpec((1, s, 2 * LANES), lambda bi, gi, qi: (bi, 0, gi))
    vspec = pl.BlockSpec((1, s // t, NSA_HEAD_DIM + V_PAD, t), lambda bi, gi, qi: (bi, 0, gi, 0))
    ospec = pl.BlockSpec((1, t, pairs * LANES), lambda bi, gi, qi: (bi, qi, gi))
    stat = lambda: pltpu.VMEM((NSA_GROUP, t), F32)
    accs = lambda: pltpu.VMEM((NSA_GROUP, NSA_HEAD_DIM + V_PAD, t), F32)
    return pl.pallas_call(
        functools.partial(_selwin_kernel, t=t),
        grid=(b, g, s // t),
        in_specs=[pl.BlockSpec((1, t, NSA_GROUP * LANES), lambda bi, gi, qi: (bi, qi, gi)),
                  kspec, vspec, kspec, vspec],
        out_specs=[ospec, ospec],
        out_shape=[jax.ShapeDtypeStruct((b, s, g * pairs * LANES), F32)] * 2,
        scratch_shapes=[stat(), accs(), stat(), accs()],
        compiler_params=_cparams(("parallel", "parallel", "arbitrary")),
        name="selwin_attention",
    )(qa, ksx, vst, kwx, vwt)


def _post_kernel(x_ref, om_ref, oc_ref, os_ref, ow_ref, g_ref, eg_ref, wo_ref, fn_ref, rwh_ref, rwl_ref, rb_ref,
                 h_ref, xn_ref, route_ref, cnt_ref, *, tm):
    i = pl.program_id(0)
    width = NSA_HEADS * NSA_HEAD_DIM
    sg = _sigmoid(g_ref[...])
    sg_hi = sg.astype(BF16)
    sg_lo = (sg - sg_hi.astype(F32)).astype(BF16)
    ge = (jnp.dot(sg_hi, eg_ref[...], preferred_element_type=F32)
          + jnp.dot(sg_lo, eg_ref[...], preferred_element_type=F32))
    o_nsa = (ge[:, 0:width] * oc_ref[...] + ge[:, width:2 * width] * os_ref[...]
             + ge[:, 2 * width:3 * width] * ow_ref[...])
    mla_w = MLA_HEADS * MLA_V_DIM
    mixed = (jnp.dot(om_ref[...].astype(BF16), wo_ref[0:mla_w, :], preferred_element_type=F32)
             + jnp.dot(o_nsa.astype(BF16), wo_ref[mla_w:mla_w + width, :], preferred_element_type=F32))
    h = x_ref[...] + mixed
    h_ref[...] = h
    xn = _rms(h, fn_ref[...])
    _lanes_to_rows(xn_ref, xn)

    x_hi = xn.astype(BF16)
    x_lo = (xn - x_hi.astype(F32)).astype(BF16)
    logits = (jnp.dot(x_hi, rwh_ref[...], preferred_element_type=F32)
              + jnp.dot(x_hi, rwl_ref[...], preferred_element_type=F32)
              + jnp.dot(x_lo, rwh_ref[...], preferred_element_type=F32)) + rb_ref[...]
    lane = lax.broadcasted_iota(I32, (tm, LANES), 1)
    lg = jnp.where(lane < N_EXPERTS, logits, -jnp.inf)
    vals, hots = [], []
    for _ in range(TOP_K):
        m = jnp.max(lg, axis=-1, keepdims=True)
        idx = jnp.min(jnp.where(lg == m, lane, LANES), axis=-1, keepdims=True)
        hot = lane == idx
        lg = jnp.where(hot, -jnp.inf, lg)
        vals.append(m)
        hots.append(hot)
    es = [jnp.exp(v - vals[0]) for v in vals]
    den = es[0] + es[1] + es[2] + es[3]

    @pl.when(i == 0)
    def _():
        cnt_ref[...] = jnp.zeros(cnt_ref.shape, F32)

    hot_all = (hots[0] | hots[1] | hots[2] | hots[3]).astype(F32)
    r = lax.broadcasted_iota(I32, (tm, tm), 0)
    c = lax.broadcasted_iota(I32, (tm, tm), 1)
    tri = (c < r).astype(BF16)
    before = jnp.dot(tri, hot_all.astype(BF16), preferred_element_type=F32) + cnt_ref[...]
    route = jnp.zeros((tm, LANES), F32)
    for k in range(TOP_K):
        e_k = jnp.sum(jnp.where(hots[k], lane, 0), axis=-1, keepdims=True).astype(F32)
        rank_k = jnp.sum(jnp.where(hots[k], before, 0.0), axis=-1, keepdims=True)
        route = (route + jnp.where(lane == k, e_k, 0.0) + jnp.where(lane == TOP_K + k, es[k] / den, 0.0)
                 + jnp.where(lane == 2 * TOP_K + k, rank_k, 0.0))
    route_ref[...] = route
    cnt_ref[...] = cnt_ref[...] + jnp.sum(hot_all, axis=0, keepdims=True)


def _post(x2, om, oc, os_, ow, gl, eg, wo, fn, rwh, rwl, rb):
    n = x2.shape[0]
    tm = POST_TM
    tok = lambda w: pl.BlockSpec((tm, w), lambda i: (i, 0))
    full = lambda a: pl.BlockSpec(a.shape, lambda i: (0,) * a.ndim)
    return pl.pallas_call(
        functools.partial(_post_kernel, tm=tm),
        grid=(n // tm,),
        in_specs=[tok(D_MODEL), tok(om.shape[1]), tok(oc.shape[1]), tok(os_.shape[1]), tok(ow.shape[1]), tok(LANES)]
        + [full(a) for a in (eg, wo, fn, rwh, rwl, rb)],
        out_specs=[tok(D_MODEL), pl.BlockSpec((tm * ROW_CHUNKS, LANES), lambda i: (i, 0)), tok(LANES),
                   pl.BlockSpec((1, LANES), lambda i: (0, 0))],
        out_shape=[jax.ShapeDtypeStruct((n, D_MODEL), F32), jax.ShapeDtypeStruct((n * ROW_CHUNKS, LANES), F32),
                   jax.ShapeDtypeStruct((n, LANES), F32), jax.ShapeDtypeStruct((1, LANES), F32)],
        compiler_params=_cparams(("arbitrary",)),
        name="post",
    )(x2, om, oc, os_, ow, gl, eg, wo, fn, rwh, rwl, rb)


def _gather_rows(idx_ref, src_hbm, dst, sem, first, count, unroll):
    def body(r, carry):
        src = pl.multiple_of(idx_ref[0, 0, first + r] * ROW_CHUNKS, ROW_CHUNKS)
        dst_row = pl.multiple_of((first + r) * ROW_CHUNKS, ROW_CHUNKS)
        pltpu.make_async_copy(src_hbm.at[pl.ds(src, ROW_CHUNKS), :], dst.at[pl.ds(dst_row, ROW_CHUNKS), :], sem).start()
        return carry
    lax.fori_loop(0, count, body, 0, unroll=unroll)


def _wait_rows(src_hbm, dst, sem, count):
    pltpu.make_async_copy(src_hbm.at[pl.ds(0, count * ROW_CHUNKS), :], dst, sem).wait()


def _rows_to_lanes(ref, lead, first, rows, dtype):
    return jnp.concatenate(
        [ref[lead + (pl.ds(first * ROW_CHUNKS + c, rows, stride=ROW_CHUNKS), slice(None))].astype(dtype)
         for c in range(ROW_CHUNKS)], axis=1)


def _lanes_to_rows(ref, val):
    for c in range(ROW_CHUNKS):
        ref[pl.ds(c, val.shape[0], stride=ROW_CHUNKS), :] = val[:, c * LANES:(c + 1) * LANES]


def _ffn_kernel(be_ref, bv_ref, idx0_ref, idx1_ref, idxn_ref, xn_hbm, w1_ref, b1_ref, w2_ref, b2_ref, y_ref, xbuf, xb_ref,
                sem, *, bm):
    del be_ref
    i = pl.program_id(0)
    slot = i % FFN_SLOTS

    @pl.when(i == 0)
    def _():
        _gather_rows(idx0_ref, xn_hbm, xbuf.at[0], sem.at[0], 0, bm, DMA_UNROLL)
        _gather_rows(idx1_ref, xn_hbm, xbuf.at[1], sem.at[1], 0, bm, DMA_UNROLL)

    @pl.when(bv_ref[i] > 0)
    def _():
        _wait_rows(xn_hbm, xbuf.at[slot], sem.at[slot], bm)
        xb_ref[...] = _rows_to_lanes(xbuf, (slot,), 0, bm, BF16)
        ahead = (i + FFN_AHEAD) % FFN_SLOTS
        _gather_rows(idxn_ref, xn_hbm, xbuf.at[ahead], sem.at[ahead], 0, bm, True)
        hcat = jnp.dot(xb_ref[...], w1_ref[0], preferred_element_type=F32) + b1_ref[0]
        a = jnp.minimum(hcat[:, :D_FF], SWIGLU_LIMIT)
        up = jnp.clip(hcat[:, D_FF:], -SWIGLU_LIMIT, SWIGLU_LIMIT)
        glu = a * _sigmoid(SWIGLU_ALPHA * a)
        y = jnp.dot(((up + 1.0) * glu).astype(BF16), w2_ref[0], preferred_element_type=F32) + b2_ref[0]
        _lanes_to_rows(y_ref, y)

    @pl.when(bv_ref[i] == 0)
    def _():
        @pl.when(bv_ref[jnp.maximum(i - FFN_AHEAD, 0)] > 0)
        def _():
            _wait_rows(xn_hbm, xbuf.at[slot], sem.at[slot], bm)

        y_ref[...] = jnp.zeros(y_ref.shape, F32)


def _ffn(block_expert, block_valid, row_tok, xn, w1, b1, w2, b2):
    nb = block_expert.shape[0]
    bm = FFN_BM
    idx3 = row_tok.reshape(nb, 1, bm)
    smem_blk = lambda fn: pl.BlockSpec((1, 1, bm), fn, memory_space=pltpu.SMEM)
    grid_spec = pltpu.PrefetchScalarGridSpec(
        num_scalar_prefetch=2,
        grid=(nb,),
        in_specs=[smem_blk(lambda i, be, bv: (0, 0, 0)),
                  smem_blk(lambda i, be, bv: (1, 0, 0)),
                  smem_blk(lambda i, be, bv: (jnp.minimum(i + FFN_AHEAD, nb - 1), 0, 0)),
                  pl.BlockSpec(memory_space=pl.ANY),
                  pl.BlockSpec((1, D_MODEL, 2 * D_FF), lambda i, be, bv: (be[i], 0, 0)),
                  pl.BlockSpec((1, 1, 2 * D_FF), lambda i, be, bv: (be[i], 0, 0)),
                  pl.BlockSpec((1, D_FF, D_MODEL), lambda i, be, bv: (be[i], 0, 0)),
                  pl.BlockSpec((1, 1, D_MODEL), lambda i, be, bv: (be[i], 0, 0))],
        out_specs=pl.BlockSpec((bm * ROW_CHUNKS, LANES), lambda i, be, bv: (i, 0)),
        scratch_shapes=[pltpu.VMEM((FFN_SLOTS, bm * ROW_CHUNKS, LANES), F32), pltpu.VMEM((bm, D_MODEL), BF16),
                        pltpu.SemaphoreType.DMA((FFN_SLOTS,))],
    )
    return pl.pallas_call(
        functools.partial(_ffn_kernel, bm=bm),
        grid_spec=grid_spec,
        out_shape=jax.ShapeDtypeStruct((nb * bm * ROW_CHUNKS, LANES), F32),
        compiler_params=_cparams(("arbitrary",)),
        name="expert_ffn",
    )(block_expert, block_valid, idx3, idx3, idx3, xn, w1, b1, w2, b2)


def _combine_kernel(idx0_ref, idxn_ref, h_ref, route_ref, fn_ref, ys_hbm, o_ref, ybuf, sem, *, tm, nt):
    i = pl.program_id(0)
    slot = i % 2
    rows = TOP_K * tm

    @pl.when(i == 0)
    def _():
        _gather_rows(idx0_ref, ys_hbm, ybuf.at[0], sem.at[0], 0, rows, DMA_UNROLL)

    @pl.when(i + 1 < nt)
    def _():
        _gather_rows(idxn_ref, ys_hbm, ybuf.at[1 - slot], sem.at[1 - slot], 0, rows, DMA_UNROLL)

    _wait_rows(ys_hbm, ybuf.at[slot], sem.at[slot], rows)
    acc = h_ref[...]
    for k in range(TOP_K):
        acc = acc + route_ref[:, TOP_K + k:TOP_K + k + 1] * _rows_to_lanes(ybuf, (slot,), k * tm, tm, F32)
    o_ref[...] = _rms(acc, fn_ref[...])


def _combine(dest_t, h, route, fn, ys):
    n = h.shape[0]
    tm = COMB_TM
    nt = n // tm
    rows = TOP_K * tm
    smem_blk = lambda fn_: pl.BlockSpec((1, 1, rows), fn_, memory_space=pltpu.SMEM)
    return pl.pallas_call(
        functools.partial(_combine_kernel, tm=tm, nt=nt),
        grid=(nt,),
        in_specs=[smem_blk(lambda i: (0, 0, 0)), smem_blk(lambda i: (jnp.minimum(i + 1, nt - 1), 0, 0)),
                  pl.BlockSpec((tm, D_MODEL), lambda i: (i, 0)), pl.BlockSpec((tm, LANES), lambda i: (i, 0)),
                  pl.BlockSpec((1, D_MODEL), lambda i: (0, 0)), pl.BlockSpec(memory_space=pl.ANY)],
        out_specs=pl.BlockSpec((tm, D_MODEL), lambda i: (i, 0)),
        out_shape=jax.ShapeDtypeStruct((n, D_MODEL), F32),
        scratch_shapes=[pltpu.VMEM((2, rows * ROW_CHUNKS, LANES), F32), pltpu.SemaphoreType.DMA((2,))],
        compiler_params=_cparams(("arbitrary",)),
        name="combine",
    )(dest_t, dest_t, h, route, fn, ys)


def _rope_tables(positions, rot, lead, period):
    inv_freq = ROPE_THETA ** (-jnp.arange(0, rot, 2, dtype=F32) / rot)
    ang = positions.astype(F32)[..., None] * inv_freq
    cos, sin = jnp.cos(ang), jnp.sin(ang)
    shape = cos.shape[:-1]
    tail = period - lead - rot
    c = jnp.concatenate([jnp.ones(shape + (lead,), F32), cos, cos, jnp.ones(shape + (tail,), F32)], axis=-1)
    s = jnp.concatenate([jnp.zeros(shape + (lead,), F32), -sin, sin, jnp.zeros(shape + (tail,), F32)], axis=-1)
    reps = LANES // period
    return jnp.tile(c, reps), jnp.tile(s, reps)


def _layout_w_in(w):
    d = w.shape[0]
    sizes = (MLA_Q_RANK, MLA_KV_RANK, MLA_ROPE_DIM, NSA_HEADS * NSA_HEAD_DIM) + (NSA_KV_HEADS * NSA_HEAD_DIM,) * 6 \
        + (3 * NSA_HEADS,)
    offs = [0]
    for sz in sizes:
        offs.append(offs[-1] + sz)
    seg = [w[:, offs[j]:offs[j + 1]] for j in range(len(sizes))]
    z = lambda n: jnp.zeros((d, n), w.dtype)
    kpe = jnp.concatenate([z(MLA_NOPE_DIM), seg[2], z(LANES - MLA_NOPE_DIM - MLA_ROPE_DIM)], axis=1)
    gates = jnp.concatenate([seg[10], z(LANES - 3 * NSA_HEADS)], axis=1)
    out = jnp.concatenate([seg[0], seg[1], kpe, seg[3]] + seg[4:10] + [gates], axis=1)
    assert out.shape[1] == C_TOTAL
    return out


def _layout_mla_up(w_q_up, w_kv_up):
    rq = w_q_up.shape[0]
    qd = MLA_NOPE_DIM + MLA_ROPE_DIM
    wq = w_q_up.reshape(rq, MLA_HEADS, qd)
    wq = jnp.pad(wq, ((0, 0), (0, 0), (0, LANES - qd))).reshape(rq, MLA_HEADS * LANES)
    rk = w_kv_up.shape[0]
    wkv = w_kv_up.reshape(rk, MLA_HEADS, MLA_NOPE_DIM + MLA_V_DIM)
    wk = jnp.pad(wkv[:, :, :MLA_NOPE_DIM], ((0, 0), (0, 0), (0, LANES - MLA_NOPE_DIM))).reshape(rk, MLA_HEADS * LANES)
    wv = wkv[:, :, MLA_NOPE_DIM:].reshape(rk, MLA_HEADS * MLA_V_DIM)
    return wq, wk, wv


def _layout_compress(cmp_pos, cmp_w1, cmp_w2):
    g = NSA_KV_HEADS
    eye = jnp.eye(g, dtype=F32)
    half = CMP_BLOCK // 2
    assert CMP_BLOCK == 2 * CMP_STRIDE

    def w1_part(w):
        return jnp.einsum('ldh,gk->lgdkh', w, eye).reshape(half * g * NSA_HEAD_DIM, g * CMP_HIDDEN)

    def pos_part(p):
        return jnp.broadcast_to(p[:, None, :], (half, g, NSA_HEAD_DIM)).reshape(1, -1)

    w1a = jnp.stack([w1_part(cmp_w1[i, :half]) for i in range(2)]).astype(BF16)
    w1b = jnp.stack([w1_part(cmp_w1[i, half:]) for i in range(2)]).astype(BF16)
    w2 = jnp.stack([jnp.einsum('hd,gk->ghkd', cmp_w2[i], eye).reshape(g * CMP_HIDDEN, g * NSA_HEAD_DIM)
                    for i in range(2)]).astype(BF16)
    pos = jnp.stack([jnp.concatenate([pos_part(cmp_pos[i, :half]), pos_part(cmp_pos[i, half:])], axis=0)
                     for i in range(2)])
    return pos, w1a, w1b, w2


def _overlap_matrix(rows, n_cmp):
    n = jnp.arange(rows)[:, None]
    j = jnp.arange(LANES)[None, :] % HALF
    start = n * CMP_STRIDE
    ov = (start < (j + 1) * SEL_BLOCK) & (start + CMP_BLOCK > j * SEL_BLOCK) & (n < n_cmp)
    return ov.astype(BF16)


def _gate_expand_matrix():
    width = NSA_HEADS * NSA_HEAD_DIM
    rows = jnp.arange(LANES)[:, None]
    cols = jnp.arange(3 * width)[None, :]
    branch, head = cols // width, (cols % width) // NSA_HEAD_DIM
    return (rows == head * 3 + branch).astype(BF16)


def kernel(x, positions, attn_norm, w_in, mla_q_norm, mla_w_q_up, mla_kv_norm, mla_w_kv_up, nsa_cmp_pos,
           nsa_cmp_w1, nsa_cmp_w2, w_out, ffn_norm, router_w, router_b, moe_w1, moe_b1, moe_w2, moe_b2, final_norm):
    b, s, d = x.shape
    n = b * s
    assert attn_norm.shape[0] == 1, "single-layer configuration"
    assert d == D_MODEL and s % SEL_BLOCK == 0 and s // SEL_BLOCK <= HALF and s % ATT_T == 0
    assert s % PROJ_TM == 0 and ATT_T % PROJ_TM == 0 and n % POST_TM == 0 and n % COMB_TM == 0

    mla_c, mla_s = _rope_tables(positions.reshape(n), MLA_ROPE_DIM, MLA_NOPE_DIM, LANES)
    nsa_c, nsa_s = _rope_tables(positions.reshape(n), NSA_ROPE_DIM, 0, NSA_HEAD_DIM)
    rows = s // CMP_STRIDE
    n_cmp = (s - CMP_BLOCK) // CMP_STRIDE + 1
    cmp_end = jnp.minimum(jnp.arange(rows) * CMP_STRIDE + CMP_BLOCK - 1, s - 1)
    cmp_c, cmp_s = _rope_tables(positions[:, cmp_end], NSA_ROPE_DIM, 0, NSA_HEAD_DIM)
    ov = _overlap_matrix(rows, n_cmp)
    eg = _gate_expand_matrix()

    x2 = x.reshape(n, d)
    win = _layout_w_in(w_in[0]).astype(BF16)
    wq, wk, wv = (w.astype(BF16) for w in _layout_mla_up(mla_w_q_up[0], mla_w_kv_up[0]))
    (qm, km, vmt, qs, kc_in, vc_in, ksx, vst, kwx, vwt, gl) = _proj(
        x2, (mla_c, mla_s, nsa_c, nsa_s), attn_norm[0][None], win, mla_q_norm[0][None], wq,
        mla_kv_norm[0][None], wk, wv, b, s)

    pos, w1a, w1b, w2c = _layout_compress(nsa_cmp_pos[0], nsa_cmp_w1[0], nsa_cmp_w2[0])
    flat = CMP_STRIDE * NSA_KV_HEADS * NSA_HEAD_DIM
    kcx, vcx = _compress(kc_in.reshape(b, rows, flat), vc_in.reshape(b, rows, flat), pos, w1a, w1b, w2c,
                         cmp_c, cmp_s)

    o_mla = _mla_attention(qm.reshape(b, s, -1), km.reshape(b, s, -1), vmt)
    o_cmp, qa = _cmp_attention(qs.reshape(b, s, -1), kcx, vcx, ov)
    o_sel, o_win = _selwin_attention(qa, ksx.reshape(b, s, -1), vst, kwx.reshape(b, s, -1), vwt)

    rw = jnp.pad(router_w[0], ((0, 0), (0, LANES - N_EXPERTS)))
    rw_hi = rw.astype(BF16)
    rw_lo = (rw - rw_hi.astype(F32)).astype(BF16)
    rb = jnp.pad(router_b[0], (0, LANES - N_EXPERTS))[None]
    hres, xn, route, counts = _post(
        x2, o_mla.reshape(n, -1), o_cmp.reshape(n, -1), o_sel.reshape(n, -1), o_win.reshape(n, -1), gl, eg,
        w_out[0].astype(BF16), ffn_norm[0][None], rw_hi, rw_lo, rb)

    top_idx = route[:, 0:TOP_K].astype(I32)
    rank = route[:, 2 * TOP_K:3 * TOP_K].astype(I32)
    cnt = counts[0, :N_EXPERTS].astype(I32)
    padded = ((cnt + FFN_BM - 1) // FFN_BM) * FFN_BM
    pend = jnp.cumsum(padded)
    pstart = pend - padded
    dest = pstart[top_idx] + rank
    nb = -(-(n * TOP_K) // FFN_BM) + N_EXPERTS + FFN_AHEAD
    block_row = jnp.arange(nb, dtype=I32) * FFN_BM
    block_expert = jnp.minimum(jnp.sum((block_row[:, None] >= pend[None, :]).astype(I32), axis=1), N_EXPERTS - 1)
    block_valid = (block_row < pend[-1]).astype(I32)
    tok_ids = jnp.broadcast_to(jnp.arange(n, dtype=I32)[:, None], (n, TOP_K))
    real = (top_idx * (2 * n) + tok_ids).reshape(-1)
    fill_end = jnp.cumsum(padded - cnt)
    fill_id = jnp.arange(nb * FFN_BM - n * TOP_K, dtype=I32)
    fill_expert = jnp.sum((fill_id[:, None] >= fill_end[None, :]).astype(I32), axis=1)
    low = jnp.sort(jnp.concatenate([real, fill_expert * (2 * n) + n + fill_id % n])) % (2 * n)
    row_tok = jnp.where(low < n, low, low - n)

    ys = _ffn(block_expert, block_valid, row_tok, xn, moe_w1[0].astype(BF16), moe_b1[0][:, None, :],
              moe_w2[0].astype(BF16), moe_b2[0][:, None, :])
    dest_t = dest.reshape(n // COMB_TM, COMB_TM, TOP_K).transpose(0, 2, 1).reshape(n // COMB_TM, 1, TOP_K * COMB_TM)
    out = _combine(dest_t, hres, route, final_norm[None], ys)
    return out.reshape(b, s, d)
```
